```python
import math
import jax, jax.numpy as jnp
from jax import lax
import numpy as np


D_MODEL = 4096
BATCH = 1
SEQ = 16384
DEPTH = 4

CHUNK = 64
Q_BLOCK = 128
HEAD_DIM = 128
N_DIFF = D_MODEL // 1024
N_MLSTM = D_MODEL // 1024
MLSTM_DV = 2 * HEAD_DIM
N_RET = D_MODEL // 1024
RET_DV = 2 * HEAD_DIM
N_SB = D_MODEL // 512
D_FF = D_MODEL
PLE_DIM = 256
PLE_GATE_RANK = 256
QK_CONV = 4
FFN_CONV = 3
ROPE_THETA = 10000.0
EPS = 1e-6
A_QK = N_DIFF * 2 * HEAD_DIM
A_V = N_DIFF * 2 * HEAD_DIM
B_QK = N_MLSTM * HEAD_DIM
B_V = N_MLSTM * MLSTM_DV
C_QK = N_RET * HEAD_DIM
C_V = N_RET * RET_DV
D_QKV = N_SB * HEAD_DIM
EVEN_SIZES = (A_QK, A_QK, A_V, B_QK, B_QK, B_V, B_V, N_MLSTM, N_MLSTM)
ODD_SIZES = (C_QK, C_QK, C_V, C_V, D_QKV, D_QKV, D_QKV)
IN_EVEN = sum(EVEN_SIZES)
IN_ODD = sum(ODD_SIZES)
EVEN_MIX = A_V + B_V
ODD_MIX = C_V + D_QKV

kernel_name = 'hybrid_streaming_encoder'


def rmsnorm(x, g):
    xf = x.astype(jnp.float32)
    y = xf * lax.rsqrt(jnp.mean(xf * xf, axis=-1, keepdims=True) + EPS)
    return (y * g.astype(jnp.float32)).astype(x.dtype)


def split_cols(x, sizes):
    outs, off = [], 0
    for s in sizes:
        outs.append(x[..., off:off + s])
        off += s
    return outs


def rope_tables(S):
    inv = ROPE_THETA ** (-jnp.arange(0, HEAD_DIM, 2, dtype=jnp.float32) / HEAD_DIM)
    ang = jnp.arange(S, dtype=jnp.float32)[:, None] * inv[None, :]
    return jnp.cos(ang)[None, :, None, :], jnp.sin(ang)[None, :, None, :]


def rope(x, cos, sin):
    x1, x2 = jnp.split(x, 2, axis=-1)
    return jnp.concatenate([x1 * cos - x2 * sin, x2 * cos + x1 * sin], axis=-1).astype(x.dtype)


def causal_dwconv(x, w, b):
    K, S = w.shape[0], x.shape[1]
    xp = jnp.pad(x, ((0, 0), (K - 1, 0), (0, 0)))
    return b + sum(w[k] * xp[:, k:k + S] for k in range(K))


def to_chunks(a):
    B, S, H = a.shape[:3]
    a = a.reshape((B, S // CHUNK, CHUNK, H) + a.shape[3:])
    return jnp.moveaxis(jnp.moveaxis(a, 1, 0), 3, 2)


def from_chunks(a):
    a = jnp.moveaxis(jnp.moveaxis(a, 2, 3), 0, 1)
    B, nc, L, H, D = a.shape
    return a.reshape(B, nc * L, H, D)


def diff_attention(q, k, v, lam, lam_init, g_sub):
    B, S, H = q.shape[:3]
    outs = []
    for bi in range(S // Q_BLOCK):
        q0 = bi * Q_BLOCK
        kend = q0 + Q_BLOCK
        s = jnp.einsum('bqhmd,bkhmd->bhmqk', q[:, q0:kend], k[:, :kend]).astype(jnp.float32) * HEAD_DIM ** -0.5
        qchunk = (q0 + jnp.arange(Q_BLOCK)) // CHUNK
        kchunk = jnp.arange(kend) // CHUNK
        s = jnp.where(kchunk[None, :] <= qchunk[:, None], s, -jnp.inf)
        pr = jax.nn.softmax(s, axis=-1)
        w = pr[:, :, 0] - lam * pr[:, :, 1]
        outs.append(jnp.einsum('bhqk,bkhe->bqhe', w.astype(v.dtype), v[:, :kend]))
    out = jnp.concatenate(outs, axis=1)
    return (rmsnorm(out, g_sub) * (1.0 - lam_init)).reshape(B, S, H * 2 * HEAD_DIM)


def mlstm(q, k, v, i_pre, f_pre, g_norm):
    B, S, H, Dk = q.shape
    Dv = v.shape[-1]
    L = CHUNK
    qc = to_chunks(q.astype(jnp.float32) * Dk ** -0.5)
    kc = to_chunks(k.astype(jnp.float32))
    vc = to_chunks(v.astype(jnp.float32))
    lic = to_chunks(i_pre.astype(jnp.float32))
    lfc = to_chunks(jax.nn.log_sigmoid(f_pre.astype(jnp.float32)))
    tril = jnp.tril(jnp.ones((L, L), bool))

    def step(carry, xs):
        C, n, m = carry
        qi, ki, vi, li, lf = xs
        b = jnp.cumsum(lf, axis=-1)
        dm = jnp.where(tril, b[..., :, None] - b[..., None, :] + li[..., None, :], -jnp.inf)
        inter = b + m[..., None]
        m_t = jnp.maximum(inter, jnp.max(dm, axis=-1))
        w_intra = jnp.exp(dm - m_t[..., None])
        w_inter = jnp.exp(inter - m_t)
        s = jnp.einsum('bhtd,bhsd->bhts', qi, ki) * w_intra
        num = jnp.einsum('bhts,bhse->bhte', s, vi) + w_inter[..., None] * jnp.einsum('bhtd,bhde->bhte', qi, C)
        nq = jnp.sum(s, axis=-1) + w_inter * jnp.einsum('bhtd,bhd->bht', qi, n)
        den = jnp.maximum(jnp.abs(nq), jnp.exp(-m_t))
        h = num / den[..., None]
        bl = b[..., -1]
        g = bl[..., None] - b + li
        m_new = jnp.maximum(bl + m, jnp.max(g, axis=-1))
        wk = jnp.exp(g - m_new[..., None])
        dec = jnp.exp(bl + m - m_new)
        C = dec[..., None, None] * C + jnp.einsum('bhs,bhsd,bhse->bhde', wk, ki, vi)
        n = dec[..., None] * n + jnp.einsum('bhs,bhsd->bhd', wk, ki)
        return (C, n, m_new), h

    init = (jnp.zeros((B, H, Dk, Dv), jnp.float32), jnp.zeros((B, H, Dk), jnp.float32), jnp.zeros((B, H), jnp.float32))
    _, h = lax.scan(step, init, (qc, kc, vc, lic, lfc))
    h = from_chunks(h)
    return rmsnorm(h, g_norm).astype(v.dtype).reshape(B, S, H * Dv)


def retention(q, k, v, g_norm):
    B, S, H, Dk = q.shape
    Dv = v.shape[-1]
    L = CHUNK
    log_g = jnp.log(1.0 - 2.0 ** (-5.0 - jnp.arange(H, dtype=jnp.float32)))
    idx = jnp.arange(L, dtype=jnp.float32)
    diff = idx[:, None] - idx[None, :]
    dmask = jnp.where(diff >= 0, jnp.exp(jnp.maximum(diff, 0.0) * log_g[:, None, None]), 0.0)
    q_dec = jnp.exp((idx + 1.0) * log_g[:, None])
    k_dec = jnp.exp((L - 1.0 - idx) * log_g[:, None])
    c_dec = jnp.exp(L * log_g)
    qc = to_chunks(q.astype(jnp.float32))
    kc = to_chunks(k.astype(jnp.float32) * Dk ** -0.5)
    vc = to_chunks(v.astype(jnp.float32))

    def step(R, xs):
        qi, ki, vi = xs
        s = jnp.einsum('bhtd,bhsd->bhts', qi, ki) * dmask
        o = jnp.einsum('bhts,bhse->bhte', s, vi) + q_dec[..., None] * jnp.einsum('bhtd,bhde->bhte', qi, R)
        R = c_dec[:, None, None] * R + jnp.einsum('bhsd,bhse->bhde', ki * k_dec[..., None], vi)
        return R, o

    _, o = lax.scan(step, jnp.zeros((B, H, Dk, Dv), jnp.float32), (qc, kc, vc))
    o = from_chunks(o)
    return rmsnorm(o, g_norm).astype(v.dtype).reshape(B, S, H * Dv)


def stick_breaking(q, k, v):
    B, S, H, Dh = q.shape
    outs = []
    for bi in range(S // Q_BLOCK):
        q0 = bi * Q_BLOCK
        kend = q0 + Q_BLOCK
        z = jnp.einsum('bqhd,bkhd->bhqk', q[:, q0:kend], k[:, :kend]).astype(jnp.float32) * Dh ** -0.5
        qpos = q0 + jnp.arange(Q_BLOCK)
        kpos = jnp.arange(kend)
        strict = kpos[None, :] < qpos[:, None]
        log_keep = jnp.where(strict, jax.nn.log_sigmoid(-z), 0.0)
        between = lax.cumsum(log_keep, axis=3, reverse=True) - log_keep
        a = jnp.where(strict, jnp.exp(jax.nn.log_sigmoid(z) + between), 0.0)
        outs.append(jnp.einsum('bhqk,bkhd->bqhd', a.astype(v.dtype), v[:, :kend]))
    return jnp.concatenate(outs, axis=1).reshape(B, S, H * Dh)


def even_mixer(u, w_in, w_out, lam_p, lam_init, g_diff, w_cqk, b_cqk, b_i, b_f, g_ml, cos, sin):
    B, S, _ = u.shape
    aq, ak, av, bq, bk, bv, bo, bi, bf = split_cols(u @ w_in, EVEN_SIZES)
    aq = rope(aq.reshape(B, S, 2 * N_DIFF, HEAD_DIM), cos, sin).reshape(B, S, N_DIFF, 2, HEAD_DIM)
    ak = rope(ak.reshape(B, S, 2 * N_DIFF, HEAD_DIM), cos, sin).reshape(B, S, N_DIFF, 2, HEAD_DIM)
    lp = lam_p.astype(jnp.float32)
    lam = jnp.exp(jnp.sum(lp[0] * lp[1])) - jnp.exp(jnp.sum(lp[2] * lp[3])) + lam_init
    ya = diff_attention(aq, ak, av.reshape(B, S, N_DIFF, 2 * HEAD_DIM), lam, lam_init, g_diff)
    qk = jax.nn.silu(causal_dwconv(jnp.concatenate([bq, bk], axis=-1), w_cqk, b_cqk))
    bq, bk = jnp.split(qk, 2, axis=-1)
    yb = mlstm(bq.reshape(B, S, N_MLSTM, HEAD_DIM), bk.reshape(B, S, N_MLSTM, HEAD_DIM),
               bv.reshape(B, S, N_MLSTM, MLSTM_DV), bi + b_i, bf + b_f, g_ml)
    yb = yb * jax.nn.sigmoid(bo)
    return jnp.concatenate([ya, yb], axis=-1) @ w_out


def odd_mixer(u, w_in, w_out, g_ret, cos, sin):
    B, S, _ = u.shape
    cq, ck, cv, cg, dq, dk, dv = split_cols(u @ w_in, ODD_SIZES)
    cq = rope(cq.reshape(B, S, N_RET, HEAD_DIM), cos, sin)
    ck = rope(ck.reshape(B, S, N_RET, HEAD_DIM), cos, sin)
    yc = retention(cq, ck, cv.reshape(B, S, N_RET, RET_DV), g_ret) * jax.nn.silu(cg)
    yd = stick_breaking(dq.reshape(B, S, N_SB, HEAD_DIM), dk.reshape(B, S, N_SB, HEAD_DIM),
                        dv.reshape(B, S, N_SB, HEAD_DIM))
    return jnp.concatenate([yc, yd], axis=-1) @ w_out


def conv_ffn(u, w_up, w_conv, b_conv, w_down):
    h = causal_dwconv(u @ w_up, w_conv, b_conv)
    a, b = jnp.split(h, 2, axis=-1)
    return (jax.nn.gelu(a) * b) @ w_down


def setup_inputs(seed: int = 0) -> dict:
    key = jax.random.key(seed)
    ks = jax.random.split(key, 26)
    NE = (DEPTH + 1) // 2
    NO = DEPTH // 2
    f32 = jnp.float32

    def dense(k, shape):
        return jax.random.normal(k, shape, f32) * shape[-2] ** -0.5

    def gain(k, shape):
        return 1.0 + 0.02 * jax.random.normal(k, shape, f32)

    def small(k, shape, s=0.02):
        return s * jax.random.normal(k, shape, f32)

    return {
        'x': jax.random.normal(ks[0], (BATCH, SEQ, D_MODEL), f32),
        'p': jax.random.normal(ks[1], (DEPTH, BATCH, SEQ, PLE_DIM), f32),
        'w_in_even': dense(ks[2], (NE, D_MODEL, IN_EVEN)),
        'w_out_even': dense(ks[3], (NE, EVEN_MIX, D_MODEL)),
        'diff_lambda': small(ks[4], (NE, 4, HEAD_DIM), 0.1),
        'g_diff_sub': gain(ks[5], (NE, 2 * HEAD_DIM)),
        'w_conv_qk': jax.random.normal(ks[6], (NE, QK_CONV, 2 * B_QK), f32) * QK_CONV ** -0.5,
        'b_conv_qk': small(ks[7], (NE, 2 * B_QK)),
        'b_igate': small(ks[8], (NE, N_MLSTM), 0.1),
        'b_fgate': jnp.linspace(3.0, 6.0, N_MLSTM, dtype=f32)[None, :] + small(ks[9], (NE, N_MLSTM), 0.1),
        'g_mlstm_head': gain(ks[10], (NE, MLSTM_DV)),
        'w_in_odd': dense(ks[11], (NO, D_MODEL, IN_ODD)),
        'w_out_odd': dense(ks[12], (NO, ODD_MIX, D_MODEL)),
        'g_ret_head': gain(ks[13], (NO, RET_DV)),
        'g_mix_pre': gain(ks[14], (DEPTH, D_MODEL)),
        'g_mix_post': gain(ks[15], (DEPTH, D_MODEL)),
        'g_ffn_pre': gain(ks[16], (DEPTH, D_MODEL)),
        'g_ffn_post': gain(ks[17], (DEPTH, D_MODEL)),
        'w_ffn_up': dense(ks[18], (DEPTH, D_MODEL, 2 * D_FF)),
        'w_ffn_conv': jax.random.normal(ks[19], (DEPTH, FFN_CONV, 2 * D_FF), f32) * FFN_CONV ** -0.5,
        'b_ffn_conv': small(ks[20], (DEPTH, 2 * D_FF)),
        'w_ffn_down': dense(ks[21], (DEPTH, D_FF, D_MODEL)),
        'g_ple': gain(ks[22], (DEPTH, D_MODEL)),
        'w_ple_gate_down': dense(ks[23], (DEPTH, D_MODEL, PLE_GATE_RANK)),
        'w_ple_gate_up': dense(ks[24], (DEPTH, PLE_GATE_RANK, D_MODEL)),
        'w_ple_proj': dense(ks[25], (DEPTH, PLE_DIM, D_MODEL)),
    }


def reference(x, p, w_in_even, w_out_even, diff_lambda, g_diff_sub, w_conv_qk, b_conv_qk, b_igate, b_fgate,
              g_mlstm_head, w_in_odd, w_out_odd, g_ret_head, g_mix_pre, g_mix_post, g_ffn_pre, g_ffn_post,
              w_ffn_up, w_ffn_conv, b_ffn_conv, w_ffn_down, g_ple, w_ple_gate_down, w_ple_gate_up, w_ple_proj):
    B, S, _ = x.shape
    cos, sin = rope_tables(S)
    h = x
    for i in range(DEPTH):
        j = i // 2
        u = rmsnorm(h, g_mix_pre[i])
        if i % 2 == 0:
            lam_init = 0.8 - 0.6 * math.exp(-0.3 * i)
            mix = even_mixer(u, w_in_even[j], w_out_even[j], diff_lambda[j], lam_init, g_diff_sub[j],
                             w_conv_qk[j], b_conv_qk[j], b_igate[j], b_fgate[j], g_mlstm_head[j], cos, sin)
        else:
            mix = odd_mixer(u, w_in_odd[j], w_out_odd[j], g_ret_head[j], cos, sin)
        h = h + rmsnorm(mix, g_mix_post[i])
        f = conv_ffn(rmsnorm(h, g_ffn_pre[i]), w_ffn_up[i], w_ffn_conv[i], b_ffn_conv[i], w_ffn_down[i])
        h = h + rmsnorm(f, g_ffn_post[i])
        gate = jax.nn.sigmoid((rmsnorm(h, g_ple[i]) @ w_ple_gate_down[i]) @ w_ple_gate_up[i])
        h = h + gate * (p[i] @ w_ple_proj[i])
    return h
```

```python
import functools
import math

import jax
import jax.numpy as jnp
from jax import lax
from jax.experimental import pallas as pl
from jax.experimental.pallas import tpu as pltpu

F32 = jnp.float32
BF16 = jnp.bfloat16

HEAD_DIM = 128
PLE_DIM = 256
ROPE_THETA = 10000.0
EPS = 1e-6
DIFF_CHUNK = 64
REC_CHUNK = 256
NEG = -1e30
VMEM_LIMIT = 56 * 1024 * 1024
LANES = 128
SUBLANES = 8


def _cparams(sem):
    return pltpu.CompilerParams(dimension_semantics=sem, vmem_limit_bytes=VMEM_LIMIT)


def _dot(a, b):
    return jnp.dot(a, b, preferred_element_type=F32)


def _dot_nt(a, b):
    return lax.dot_general(a, b, (((1,), (1,)), ((), ())), preferred_element_type=F32)


def _rms(x, g):
    return x * lax.rsqrt(jnp.mean(x * x, axis=-1, keepdims=True) + EPS) * g


def _log_sigmoid(x):
    return jnp.minimum(x, 0.0) - jnp.log1p(jnp.exp(-jnp.abs(x)))


def _sigmoid(x):
    return 1.0 / (1.0 + jnp.exp(-x))


def _tile(n, pref):
    t = min(n, pref)
    assert n % t == 0, (n, pref)
    return t


def _norm_body(x_ref, g_ref, u_ref):
    u_ref[...] = _rms(x_ref[...], g_ref[...]).astype(BF16)


def first_norm(x, g):
    M, D = x.shape
    tm = _tile(M, 256)
    return pl.pallas_call(
        _norm_body,
        grid=(M // tm,),
        in_specs=[pl.BlockSpec((tm, D), lambda i: (i, 0)),
                  pl.BlockSpec((1, D), lambda i: (0, 0))],
        out_specs=pl.BlockSpec((tm, D), lambda i: (i, 0)),
        out_shape=jax.ShapeDtypeStruct((M, D), BF16),
        compiler_params=_cparams(("parallel",)),
        name="first_norm",
    )(x, g.reshape(1, D))


def _residual_body(h_ref, y_ref, gp_ref, gn_ref, ho_ref, u_ref):
    hn = h_ref[...] + _rms(y_ref[...], gp_ref[...])
    ho_ref[...] = hn
    u_ref[...] = _rms(hn, gn_ref[...]).astype(BF16)


def residual_norm(h, y, g_post, g_next):
    M, D = h.shape
    tm = _tile(M, 256)
    row = pl.BlockSpec((tm, D), lambda i: (i, 0))
    vec = pl.BlockSpec((1, D), lambda i: (0, 0))
    return pl.pallas_call(
        _residual_body,
        grid=(M // tm,),
        in_specs=[row, row, vec, vec],
        out_specs=[row, row],
        out_shape=[jax.ShapeDtypeStruct((M, D), F32), jax.ShapeDtypeStruct((M, D), BF16)],
        compiler_params=_cparams(("parallel",)),
        name="residual_norm",
    )(h, y, g_post.reshape(1, D), g_next.reshape(1, D))


def _ple_body(t_ref, p_ref, wu_ref, wp_ref, h_ref, gn_ref, ho_ref, *maybe_u_ref):
    gate = _sigmoid(_dot(t_ref[...], wu_ref[...]))
    inj = _dot(p_ref[...].astype(BF16), wp_ref[...])
    hn = h_ref[...] + gate * inj
    ho_ref[...] = hn
    if maybe_u_ref:
        maybe_u_ref[0][...] = _rms(hn, gn_ref[...]).astype(BF16)


def ple_inject(t, p, w_up, w_proj, h, g_next):
    M, D = h.shape
    R = t.shape[1]
    P = p.shape[1]
    tm = _tile(M, 256)
    row = pl.BlockSpec((tm, D), lambda i: (i, 0))
    vec = pl.BlockSpec((1, D), lambda i: (0, 0))
    want_u = g_next is not None
    out_specs = [row, row] if want_u else [row]
    out_shape = [jax.ShapeDtypeStruct((M, D), F32)]
    if want_u:
        out_shape.append(jax.ShapeDtypeStruct((M, D), BF16))
    gn = g_next if want_u else jnp.ones((D,), F32)
    res = pl.pallas_call(
        _ple_body,
        grid=(M // tm,),
        in_specs=[pl.BlockSpec((tm, R), lambda i: (i, 0)),
                  pl.BlockSpec((tm, P), lambda i: (i, 0)),
                  pl.BlockSpec((R, D), lambda i: (0, 0)),
                  pl.BlockSpec((P, D), lambda i: (0, 0)),
                  row, vec],
        out_specs=out_specs,
        out_shape=out_shape,
        compiler_params=_cparams(("parallel",)),
        name="ple_inject",
    )(t, p, w_up, w_proj, h, gn.reshape(1, D))
    return (res[0], res[1]) if want_u else (res[0], None)


def _matmul_body(*refs, n_pairs):
    o_ref = refs[-1]
    acc = _dot(refs[0][...], refs[1][...])
    for q in range(1, n_pairs):
        acc = acc + _dot(refs[2 * q][...], refs[2 * q + 1][...])
    o_ref[...] = acc.astype(o_ref.dtype)


def matmul(pairs, out_dtype, tm=512, tn=1024, name="matmul"):
    M = pairs[0][0].shape[0]
    N = pairs[0][1].shape[1]
    tm = _tile(M, tm)
    tn = _tile(N, tn)
    in_specs, args = [], []
    for x, w in pairs:
        K = x.shape[1]
        in_specs += [pl.BlockSpec((tm, K), lambda i, j: (i, 0)),
                     pl.BlockSpec((K, tn), lambda i, j: (0, j))]
        args += [x, w]
    return pl.pallas_call(
        functools.partial(_matmul_body, n_pairs=len(pairs)),
        grid=(M // tm, N // tn),
        in_specs=in_specs,
        out_specs=pl.BlockSpec((tm, tn), lambda i, j: (i, j)),
        out_shape=jax.ShapeDtypeStruct((M, N), out_dtype),
        compiler_params=_cparams(("parallel", "arbitrary")),
        name=name,
    )(*args)


def _shifted_rows(xs_ref, carry_ref, acc, first_tile, shift_count):
    tm = acc.shape[0]

    @pl.when(first_tile)
    def _():
        carry_ref[...] = jnp.zeros_like(carry_ref)

    xs_ref[0:SUBLANES, :] = carry_ref[...]
    xs_ref[SUBLANES:SUBLANES + tm, :] = acc
    carry_ref[...] = acc[tm - SUBLANES:, :]
    return [xs_ref[SUBLANES - d:SUBLANES - d + tm, :] for d in range(1, shift_count + 1)]


def _causal_conv(xs_ref, carry_ref, acc, first_tile, w, b):
    K = w.shape[0]
    shifted = _shifted_rows(xs_ref, carry_ref, acc, first_tile, K - 1)
    y = b + w[K - 1:K, :] * acc
    for d in range(1, K):
        y = y + w[K - 1 - d:K - d, :] * shifted[d - 1]
    return y


def _rope_store(o_ref, a, cos, sin):
    for hh in range(a.shape[1] // HEAD_DIM):
        sl = slice(hh * HEAD_DIM, (hh + 1) * HEAD_DIM)
        x = a[:, sl]
        o_ref[:, sl] = (x * cos + pltpu.roll(x, HEAD_DIM // 2, 1) * sin).astype(o_ref.dtype)


def _in_proj_body(u_ref, w_ref, cs_ref, cos_ref, sin_ref, cw_ref, cb_ref, o_ref, carry_ref, xs_ref,
                  *, n_rope, conv_tile):
    i = pl.program_id(0)
    j = pl.program_id(1)
    acc = _dot(u_ref[...], w_ref[...])

    @pl.when(j < n_rope)
    def _():
        _rope_store(o_ref, acc * cs_ref[...], cos_ref[...], sin_ref[...])

    @pl.when(j == conv_tile)
    def _():
        y = _causal_conv(xs_ref, carry_ref, acc, i == 0, cw_ref[...], cb_ref[...])
        o_ref[...] = (y * _sigmoid(y) * cs_ref[...]).astype(o_ref.dtype)

    @pl.when(jnp.logical_and(j >= n_rope, j != conv_tile))
    def _():
        o_ref[...] = (acc * cs_ref[...]).astype(o_ref.dtype)


def in_proj(u, w, colscale, cosf, sins, conv_w, conv_b, n_rope, conv_tile, tn):
    M, K = u.shape
    N = w.shape[1]
    tm = _tile(M, 512)
    kc = conv_w.shape[0]
    return pl.pallas_call(
        functools.partial(_in_proj_body, n_rope=n_rope, conv_tile=conv_tile),
        grid=(M // tm, N // tn),
        in_specs=[pl.BlockSpec((tm, K), lambda i, j: (i, 0)),
                  pl.BlockSpec((K, tn), lambda i, j: (0, j)),
                  pl.BlockSpec((1, tn), lambda i, j: (0, j)),
                  pl.BlockSpec((tm, HEAD_DIM), lambda i, j: (i, 0)),
                  pl.BlockSpec((tm, HEAD_DIM), lambda i, j: (i, 0)),
                  pl.BlockSpec((kc, tn), lambda i, j: (0, 0)),
                  pl.BlockSpec((1, tn), lambda i, j: (0, 0))],
        out_specs=pl.BlockSpec((tm, tn), lambda i, j: (i, j)),
        out_shape=jax.ShapeDtypeStruct((M, N), BF16),
        scratch_shapes=[pltpu.VMEM((SUBLANES, tn), F32), pltpu.VMEM((tm + SUBLANES, tn), F32)],
        compiler_params=_cparams(("arbitrary", "arbitrary")),
        name="in_proj",
    )(u, w, colscale, cosf, sins, conv_w, conv_b)


def _gate_proj_body(u_ref, w_ref, wt_ref, o_ref, ot_ref):
    u = u_ref[...]
    o_ref[...] = _dot(u, w_ref[...])
    ot_ref[...] = _dot_nt(wt_ref[...], u)


def gate_proj(u, w_pad, wt_pad):
    M, K = u.shape
    tm = _tile(M, 512)
    return pl.pallas_call(
        _gate_proj_body,
        grid=(M // tm,),
        in_specs=[pl.BlockSpec((tm, K), lambda i: (i, 0)),
                  pl.BlockSpec((K, LANES), lambda i: (0, 0)),
                  pl.BlockSpec((SUBLANES, K), lambda i: (0, 0))],
        out_specs=[pl.BlockSpec((tm, LANES), lambda i: (i, 0)),
                   pl.BlockSpec((SUBLANES, tm), lambda i: (0, i))],
        out_shape=[jax.ShapeDtypeStruct((M, LANES), F32), jax.ShapeDtypeStruct((SUBLANES, M), F32)],
        compiler_params=_cparams(("parallel",)),
        name="gate_proj",
    )(u, w_pad, wt_pad)


def _gelu_tanh(x):
    return 0.5 * x * (1.0 + jnp.tanh(math.sqrt(2.0 / math.pi) * (x + 0.044715 * (x * x * x))))


def _ffn_up_body(u_ref, wa_ref, wb_ref, cwa_ref, cba_ref, cwb_ref, cbb_ref, o_ref,
                 carry_a, carry_b, xs_a, xs_b):
    i = pl.program_id(0)
    j = pl.program_id(1)
    u = u_ref[...]
    a = _causal_conv(xs_a, carry_a.at[j], _dot(u, wa_ref[...]), i == 0, cwa_ref[...], cba_ref[...])
    b = _causal_conv(xs_b, carry_b.at[j], _dot(u, wb_ref[...]), i == 0, cwb_ref[...], cbb_ref[...])
    o_ref[...] = (_gelu_tanh(a) * b).astype(o_ref.dtype)


def ffn_up(u, w_up, conv_w, conv_b):
    M, K = u.shape
    F = w_up.shape[1] // 2
    tm = _tile(M, 512)
    tn = _tile(F, 512)
    nj = F // tn
    kc = conv_w.shape[0]
    cb = conv_b.reshape(1, 2 * F)
    return pl.pallas_call(
        _ffn_up_body,
        grid=(M // tm, nj),
        in_specs=[pl.BlockSpec((tm, K), lambda i, j: (i, 0)),
                  pl.BlockSpec((K, tn), lambda i, j: (0, j)),
                  pl.BlockSpec((K, tn), lambda i, j: (0, j + nj)),
                  pl.BlockSpec((kc, tn), lambda i, j: (0, j)),
                  pl.BlockSpec((1, tn), lambda i, j: (0, j)),
                  pl.BlockSpec((kc, tn), lambda i, j: (0, j + nj)),
                  pl.BlockSpec((1, tn), lambda i, j: (0, j + nj))],
        out_specs=pl.BlockSpec((tm, tn), lambda i, j: (i, j)),
        out_shape=jax.ShapeDtypeStruct((M, F), BF16),
        scratch_shapes=[pltpu.VMEM((nj, SUBLANES, tn), F32), pltpu.VMEM((nj, SUBLANES, tn), F32),
                        pltpu.VMEM((tm + SUBLANES, tn), F32), pltpu.VMEM((tm + SUBLANES, tn), F32)],
        compiler_params=_cparams(("arbitrary", "arbitrary")),
        name="ffn_up",
    )(u, w_up, w_up, conv_w, cb, conv_w, cb)


def _diff_attn_body(lam_ref, q_ref, k_ref, v_ref, g_ref, o_ref, acc0_ref, acc1_ref, *, tq, lam_init):
    qi = pl.program_id(1)
    q = q_ref[...]
    qs = (q[:, :HEAD_DIM], q[:, HEAD_DIM:])
    accs = (acc0_ref, acc1_ref)

    row = lax.broadcasted_iota(jnp.int32, (tq, tq), 0) // DIFF_CHUNK
    col = lax.broadcasted_iota(jnp.int32, (tq, tq), 1) // DIFF_CHUNK
    visible = col <= row
    start = pl.multiple_of(qi * tq, tq)
    kd = k_ref[pl.ds(start, tq), :]
    vd = v_ref[pl.ds(start, tq), :]
    init = []
    for mp in range(2):
        s = jnp.where(visible, _dot_nt(qs[mp], kd[:, mp * HEAD_DIM:(mp + 1) * HEAD_DIM]), NEG)
        m = jnp.max(s, axis=-1, keepdims=True)
        p = jnp.exp(s - m)
        accs[mp][...] = _dot(p.astype(BF16), vd)
        init += [m, jnp.sum(p, axis=-1, keepdims=True)]

    def step(kb, carry):
        off = pl.multiple_of(kb * tq, tq)
        kk = k_ref[pl.ds(off, tq), :]
        vv = v_ref[pl.ds(off, tq), :]
        out = []
        for mp in range(2):
            m, l = carry[2 * mp], carry[2 * mp + 1]
            s = _dot_nt(qs[mp], kk[:, mp * HEAD_DIM:(mp + 1) * HEAD_DIM])
            m_new = jnp.maximum(m, jnp.max(s, axis=-1, keepdims=True))
            alpha = jnp.exp(m - m_new)
            p = jnp.exp(s - m_new)
            accs[mp][...] = alpha * accs[mp][...] + _dot(p.astype(BF16), vv)
            out += [m_new, alpha * l + jnp.sum(p, axis=-1, keepdims=True)]
        return tuple(out)

    _, l0, _, l1 = lax.fori_loop(0, qi, step, tuple(init))

    lp = lam_ref[...]
    lam = (jnp.exp(jnp.sum(lp[0:1, :] * lp[1:2, :], axis=-1, keepdims=True))
           - jnp.exp(jnp.sum(lp[2:3, :] * lp[3:4, :], axis=-1, keepdims=True)) + lam_init)
    out = acc0_ref[...] / l0 - lam * (acc1_ref[...] / l1)
    o_ref[...] = (_rms(out, g_ref[...]) * (1.0 - lam_init)).astype(o_ref.dtype)


def diff_attention(proj, lam_p, g_sub, lam_init, n_heads, q_col, k_col, v_col):
    S = proj.shape[0]
    W = 2 * HEAD_DIM
    tq = _tile(S, 512)
    return pl.pallas_call(
        functools.partial(_diff_attn_body, tq=tq, lam_init=lam_init),
        grid=(n_heads, S // tq),
        in_specs=[pl.BlockSpec((4, HEAD_DIM), lambda h, i: (0, 0)),
                  pl.BlockSpec((tq, W), lambda h, i: (i, q_col // W + h)),
                  pl.BlockSpec((S, W), lambda h, i: (0, k_col // W + h)),
                  pl.BlockSpec((S, W), lambda h, i: (0, v_col // W + h)),
                  pl.BlockSpec((1, W), lambda h, i: (0, 0))],
        out_specs=pl.BlockSpec((tq, W), lambda h, i: (i, h)),
        out_shape=jax.ShapeDtypeStruct((S, n_heads * W), BF16),
        scratch_shapes=[pltpu.VMEM((tq, W), F32), pltpu.VMEM((tq, W), F32)],
        compiler_params=_cparams(("parallel", "parallel")),
        name="diff_attention",
    )(lam_p, proj, proj, proj, g_sub.reshape(1, W))


def _stick_breaking_body(q_ref, k_ref, v_ref, o_ref, acc_ref, *, tq):
    qi = pl.program_id(1)
    q = q_ref[...]
    row = lax.broadcasted_iota(jnp.int32, (tq, tq), 0)
    col = lax.broadcasted_iota(jnp.int32, (tq, tq), 1)
    strict = col < row
    suffix_ones = (row >= col).astype(BF16)

    def block(kb, run, diag):
        off = pl.multiple_of(kb * tq, tq)
        z = _dot_nt(q, k_ref[pl.ds(off, tq), :])
        log_keep = -(jnp.maximum(z, 0.0) + jnp.log1p(jnp.exp(-jnp.abs(z))))
        if diag:
            log_keep = jnp.where(strict, log_keep, 0.0)
        hi = log_keep.astype(BF16)
        lo = (log_keep - hi.astype(F32)).astype(BF16)
        cum = _dot(hi, suffix_ones) + _dot(lo, suffix_ones)
        a = jnp.exp(z + cum + run)
        if diag:
            a = jnp.where(strict, a, 0.0)
        acc_ref[...] += _dot(a.astype(BF16), v_ref[pl.ds(off, tq), :])
        return run + cum[:, 0:1]

    acc_ref[...] = jnp.zeros_like(acc_ref)
    run = block(qi, jnp.zeros((tq, 1), F32), True)
    lax.fori_loop(0, qi, lambda n, r: block(qi - 1 - n, r, False), run)
    o_ref[...] = acc_ref[...].astype(o_ref.dtype)


def stick_breaking(proj, n_heads, q_col, k_col, v_col):
    S = proj.shape[0]
    W = HEAD_DIM
    tq = _tile(S, 256)
    return pl.pallas_call(
        functools.partial(_stick_breaking_body, tq=tq),
        grid=(n_heads, S // tq),
        in_specs=[pl.BlockSpec((tq, W), lambda h, i: (i, q_col // W + h)),
                  pl.BlockSpec((S, W), lambda h, i: (0, k_col // W + h)),
                  pl.BlockSpec((S, W), lambda h, i: (0, v_col // W + h))],
        out_specs=pl.BlockSpec((tq, W), lambda h, i: (i, h)),
        out_shape=jax.ShapeDtypeStruct((S, n_heads * W), BF16),
        scratch_shapes=[pltpu.VMEM((tq, W), F32)],
        compiler_params=_cparams(("parallel", "parallel")),
        name="stick_breaking",
    )(proj, proj, proj)


def _split_dot(x, ones):
    hi = x.astype(BF16)
    lo = (x - hi.astype(F32)).astype(BF16)
    return hi, lo


def _mlstm_body(q_ref, k_ref, v_ref, og_ref, gc_ref, gr_ref, bc_ref, br_ref, gn_ref, o_ref,
                c_ref, n_ref, m_ref, *, n_heads, L):
    dv = 2 * HEAD_DIM

    @pl.when(pl.program_id(0) == 0)
    def _():
        c_ref[...] = jnp.zeros_like(c_ref)
        n_ref[...] = jnp.zeros_like(n_ref)
        m_ref[...] = jnp.zeros_like(m_ref)

    row = lax.broadcasted_iota(jnp.int32, (L, L), 0)
    col = lax.broadcasted_iota(jnp.int32, (L, L), 1)
    tril = col <= row
    tril_ones = tril.astype(BF16)
    triu_ones = (row <= col).astype(BF16)

    gc = gc_ref[...] + bc_ref[...]
    gr = gr_ref[...] + br_ref[...]
    lfc = _log_sigmoid(gc)
    lfr = _log_sigmoid(gr)
    hi, lo = _split_dot(lfc, None)
    cum_c = _dot(tril_ones, hi) + _dot(tril_ones, lo)
    hi, lo = _split_dot(lfr, None)
    cum_r = _dot(hi, triu_ones) + _dot(lo, triu_ones)

    gnorm = gn_ref[...]
    for h in range(n_heads):
        b_c = cum_c[:, n_heads + h:n_heads + h + 1]
        li_c = gc[:, h:h + 1]
        b_r = cum_r[n_heads + h:n_heads + h + 1, :]
        li_r = gr[h:h + 1, :]
        m_prev = m_ref[:, h:h + 1]

        dm = jnp.where(tril, b_c - b_r + li_r, NEG)
        inter = b_c + m_prev
        m_t = jnp.maximum(inter, jnp.max(dm, axis=-1, keepdims=True))
        w_intra = jnp.exp(dm - m_t)
        w_inter = jnp.exp(inter - m_t)

        qh = q_ref[:, h * HEAD_DIM:(h + 1) * HEAD_DIM]
        kh = k_ref[:, h * HEAD_DIM:(h + 1) * HEAD_DIM]
        vh = v_ref[:, h * dv:(h + 1) * dv]
        s = _dot_nt(qh, kh) * w_intra
        num = _dot(s.astype(BF16), vh) + w_inter * _dot(qh, c_ref[h].astype(BF16))
        nq = (jnp.sum(s, axis=-1, keepdims=True)
              + w_inter * jnp.sum(qh.astype(F32) * n_ref[h], axis=-1, keepdims=True))
        den = jnp.maximum(jnp.abs(nq), jnp.exp(-m_t))
        hh = num / den

        bl = b_c[L - 1:L, :]
        g_c = bl - b_c + li_c
        m_new = jnp.maximum(bl + m_prev, jnp.max(g_c, axis=0, keepdims=True))
        wk = jnp.exp(g_c - m_new)
        dec = jnp.exp(bl + m_prev - m_new)
        kw = kh.astype(F32) * wk
        c_ref[h] = dec * c_ref[h] + _dot(kw.T.astype(BF16), vh)
        n_ref[h] = dec * n_ref[h] + jnp.sum(kw, axis=0, keepdims=True)
        m_ref[:, h:h + 1] = m_new

        og = og_ref[:, h * dv:(h + 1) * dv].astype(F32)
        o_ref[:, h * dv:(h + 1) * dv] = (_rms(hh, gnorm) * _sigmoid(og)).astype(o_ref.dtype)


def mlstm(proj, gates_c, gates_r, bias_c, bias_r, g_norm, n_heads, q_col, k_col, v_col, og_col):
    S = proj.shape[0]
    L = _tile(S, REC_CHUNK)
    qk_w = n_heads * HEAD_DIM
    v_w = n_heads * 2 * HEAD_DIM
    return pl.pallas_call(
        functools.partial(_mlstm_body, n_heads=n_heads, L=L),
        grid=(S // L,),
        in_specs=[pl.BlockSpec((L, qk_w), lambda c: (c, q_col // qk_w)),
                  pl.BlockSpec((L, qk_w), lambda c: (c, k_col // qk_w)),
                  pl.BlockSpec((L, v_w), lambda c: (c, v_col // v_w)),
                  pl.BlockSpec((L, v_w), lambda c: (c, og_col // v_w)),
                  pl.BlockSpec((L, LANES), lambda c: (c, 0)),
                  pl.BlockSpec((SUBLANES, L), lambda c: (0, c)),
                  pl.BlockSpec((1, LANES), lambda c: (0, 0)),
                  pl.BlockSpec((SUBLANES, 1), lambda c: (0, 0)),
                  pl.BlockSpec((1, 2 * HEAD_DIM), lambda c: (0, 0))],
        out_specs=pl.BlockSpec((L, v_w), lambda c: (c, 0)),
        out_shape=jax.ShapeDtypeStruct((S, v_w), BF16),
        scratch_shapes=[pltpu.VMEM((n_heads, HEAD_DIM, 2 * HEAD_DIM), F32),
                        pltpu.VMEM((n_heads, 1, HEAD_DIM), F32),
                        pltpu.VMEM((1, LANES), F32)],
        compiler_params=_cparams(("arbitrary",)),
        name="mlstm",
    )(proj, proj, proj, proj, gates_c, gates_r, bias_c, bias_r, g_norm.reshape(1, 2 * HEAD_DIM))


def _retention_body(q_ref, k_ref, v_ref, cg_ref, gn_ref, o_ref, r_ref, dmask_ref, *, n_heads, L):
    dv = 2 * HEAD_DIM
    log_g = [math.log(1.0 - 2.0 ** (-5.0 - h)) for h in range(n_heads)]

    @pl.when(pl.program_id(0) == 0)
    def _():
        r_ref[...] = jnp.zeros_like(r_ref)
        diff = (lax.broadcasted_iota(jnp.int32, (L, L), 0)
                - lax.broadcasted_iota(jnp.int32, (L, L), 1)).astype(F32)
        for h in range(n_heads):
            dmask_ref[h] = jnp.where(diff >= 0, jnp.exp(jnp.maximum(diff, 0.0) * log_g[h]), 0.0)

    idx = lax.broadcasted_iota(jnp.int32, (L, 1), 0).astype(F32)
    gnorm = gn_ref[...]
    for h in range(n_heads):
        q_dec = jnp.exp((idx + 1.0) * log_g[h])
        k_dec = jnp.exp((L - 1.0 - idx) * log_g[h])
        c_dec = math.exp(L * log_g[h])
        qh = q_ref[:, h * HEAD_DIM:(h + 1) * HEAD_DIM]
        kh = k_ref[:, h * HEAD_DIM:(h + 1) * HEAD_DIM]
        vh = v_ref[:, h * dv:(h + 1) * dv]
        s = _dot_nt(qh, kh) * dmask_ref[h]
        o = _dot(s.astype(BF16), vh) + q_dec * _dot(qh, r_ref[h].astype(BF16))
        kd = kh.astype(F32) * k_dec
        r_ref[h] = c_dec * r_ref[h] + _dot(kd.T.astype(BF16), vh)
        cg = cg_ref[:, h * dv:(h + 1) * dv].astype(F32)
        o_ref[:, h * dv:(h + 1) * dv] = (_rms(o, gnorm) * (cg * _sigmoid(cg))).astype(o_ref.dtype)


def retention(proj, g_norm, n_heads, q_col, k_col, v_col, cg_col):
    S = proj.shape[0]
    L = _tile(S, REC_CHUNK)
    qk_w = n_heads * HEAD_DIM
    v_w = n_heads * 2 * HEAD_DIM
    return pl.pallas_call(
        functools.partial(_retention_body, n_heads=n_heads, L=L),
        grid=(S // L,),
        in_specs=[pl.BlockSpec((L, qk_w), lambda c: (c, q_col // qk_w)),
                  pl.BlockSpec((L, qk_w), lambda c: (c, k_col // qk_w)),
                  pl.BlockSpec((L, v_w), lambda c: (c, v_col // v_w)),
                  pl.BlockSpec((L, v_w), lambda c: (c, cg_col // v_w)),
                  pl.BlockSpec((1, 2 * HEAD_DIM), lambda c: (0, 0))],
        out_specs=pl.BlockSpec((L, v_w), lambda c: (c, 0)),
        out_shape=jax.ShapeDtypeStruct((S, v_w), BF16),
        scratch_shapes=[pltpu.VMEM((n_heads, HEAD_DIM, 2 * HEAD_DIM), F32),
                        pltpu.VMEM((n_heads, L, L), F32)],
        compiler_params=_cparams(("arbitrary",)),
        name="retention",
    )(proj, proj, proj, proj, g_norm.reshape(1, 2 * HEAD_DIM))


def _rope_tables(S):
    inv = ROPE_THETA ** (-jnp.arange(0, HEAD_DIM, 2, dtype=F32) / HEAD_DIM)
    ang = jnp.arange(S, dtype=F32)[:, None] * inv[None, :]
    cos, sin = jnp.cos(ang), jnp.sin(ang)
    return jnp.concatenate([cos, cos], axis=-1), jnp.concatenate([-sin, sin], axis=-1)


def _col_scale(n, scaled_ranges):
    cs = jnp.ones((1, n), F32)
    for lo, hi in scaled_ranges:
        cs = cs.at[:, lo:hi].set(HEAD_DIM ** -0.5)
    return cs


def kernel(x, p, w_in_even, w_out_even, diff_lambda, g_diff_sub, w_conv_qk, b_conv_qk, b_igate, b_fgate, g_mlstm_head, w_in_odd, w_out_odd, g_ret_head, g_mix_pre, g_mix_post, g_ffn_pre, g_ffn_post, w_ffn_up, w_ffn_conv, b_ffn_conv, w_ffn_down, g_ple, w_ple_gate_down, w_ple_gate_up, w_ple_proj):
    B, S, D = x.shape
    assert B == 1
    depth = g_mix_pre.shape[0]
    n4 = D // 1024
    n_sb = D // 512
    qk4 = n4 * HEAD_DIM
    v4 = n4 * 2 * HEAD_DIM
    sbw = n_sb * HEAD_DIM
    assert 2 * qk4 == v4 == sbw
    tn = v4

    cosf, sins = _rope_tables(S)
    even_main = 6 * v4
    cs_even = _col_scale(even_main, [(0, v4), (3 * v4, 3 * v4 + qk4)])
    cs_odd = _col_scale(6 * v4, [(qk4, 2 * qk4), (3 * v4, 4 * v4)])
    no_conv_w = jnp.zeros((1, tn), F32)
    no_conv_b = jnp.zeros((1, tn), F32)

    h = x.reshape(S, D)
    u = first_norm(h, g_mix_pre[0])
    for i in range(depth):
        j = i // 2
        if i % 2 == 0:
            lam_init = 0.8 - 0.6 * math.exp(-0.3 * i)
            w_in = w_in_even[j]
            proj = in_proj(u, w_in[:, :even_main].astype(BF16), cs_even, cosf, sins,
                           w_conv_qk[j], b_conv_qk[j].reshape(1, tn), n_rope=2, conv_tile=3, tn=tn)
            w_gate = w_in[:, even_main:]
            w_gate_c = jnp.pad(w_gate, ((0, 0), (0, LANES - 2 * n4))).astype(BF16)
            w_gate_r = jnp.pad(w_gate.T, ((0, SUBLANES - 2 * n4), (0, 0))).astype(BF16)
            gates_c, gates_r = gate_proj(u, w_gate_c, w_gate_r)
            bias = jnp.concatenate([b_igate[j], b_fgate[j]])
            bias_c = jnp.pad(bias, (0, LANES - 2 * n4)).reshape(1, LANES)
            bias_r = jnp.pad(bias, (0, SUBLANES - 2 * n4)).reshape(SUBLANES, 1)
            ya = diff_attention(proj, diff_lambda[j], g_diff_sub[j], lam_init, n4,
                                q_col=0, k_col=v4, v_col=2 * v4)
            yb = mlstm(proj, gates_c, gates_r, bias_c, bias_r, g_mlstm_head[j], n4,
                       q_col=3 * v4, k_col=3 * v4 + qk4, v_col=4 * v4, og_col=5 * v4)
            w_out = w_out_even[j].astype(BF16)
        else:
            proj = in_proj(u, w_in_odd[j].astype(BF16), cs_odd, cosf, sins,
                           no_conv_w, no_conv_b, n_rope=1, conv_tile=-1, tn=tn)
            ya = retention(proj, g_ret_head[j], n4, q_col=0, k_col=qk4, v_col=v4, cg_col=2 * v4)
            yb = stick_breaking(proj, n_sb, q_col=3 * v4, k_col=4 * v4, v_col=5 * v4)
            w_out = w_out_odd[j].astype(BF16)
        mix = matmul([(ya, w_out[:v4]), (yb, w_out[v4:])], F32, name="out_proj")
        h, u = residual_norm(h, mix, g_mix_post[i], g_ffn_pre[i])
        hid = ffn_up(u, w_ffn_up[i].astype(BF16), w_ffn_conv[i], b_ffn_conv[i])
        f = matmul([(hid, w_ffn_down[i].astype(BF16))], F32, name="ffn_down")
        h, u = residual_norm(h, f, g_ffn_post[i], g_ple[i])
        t = matmul([(u, w_ple_gate_down[i].astype(BF16))], BF16, tn=256, name="ple_down")
        g_next = g_mix_pre[i + 1] if i + 1 < depth else None
        h, u = ple_inject(t, p[i, 0], w_ple_gate_up[i].astype(BF16), w_ple_proj[i].astype(BF16), h, g_next)
    return h.reshape(B, S, D)
```

```python
import functools
import math

import jax
import jax.numpy as jnp
from jax import lax
from jax.experimental import pallas as pl
from jax.experimental.pallas import tpu as pltpu

F32 = jnp.float32
BF16 = jnp.bfloat16

HEAD_DIM = 128
ROPE_THETA = 10000.0
EPS = 1e-6
DIFF_CHUNK = 64
REC_CHUNK = 256
NEG = -1e30
LOG2E = math.log2(math.e)
DIFF_MIN_NORMALISER = 2.0 ** -80
SB_LOG_FLOOR = -120.0
VMEM_LIMIT = 56 * 1024 * 1024
LANES = 128
SUBLANES = 8
MXU_COLS = 256
ROW_TILE = 1024
COL_TILE = 1024
ROW_PIECE = 512


def _cparams(sem):
    return pltpu.CompilerParams(dimension_semantics=sem, vmem_limit_bytes=VMEM_LIMIT)


def _dot(a, b):
    return jnp.dot(a, b, preferred_element_type=F32)


def _dot_nt(a, b):
    return lax.dot_general(a, b, (((1,), (1,)), ((), ())), preferred_element_type=F32)


def _rms(x, g):
    return x * lax.rsqrt(jnp.mean(x * x, axis=-1, keepdims=True) + EPS) * g


def _log_sigmoid(x):
    return jnp.minimum(x, 0.0) - jnp.log1p(jnp.exp(-jnp.abs(x)))


def _sigmoid(x):
    return 1.0 / (1.0 + jnp.exp(-x))


def _tile(n, pref):
    if n <= pref:
        return n
    t = pref - pref % MXU_COLS
    while n % t:
        t -= MXU_COLS
    assert t > 0, (n, pref)
    return t


def _col_subtiles(tn):
    w = min(tn, MXU_COLS)
    assert tn % w == 0
    return [slice(c * w, (c + 1) * w) for c in range(tn // w)]


def _norm_body(x_ref, g_ref, u_ref):
    u_ref[...] = _rms(x_ref[...], g_ref[...]).astype(BF16)


def first_norm(x, g):
    M, D = x.shape
    tm = _tile(M, 256)
    return pl.pallas_call(
        _norm_body,
        grid=(M // tm,),
        in_specs=[pl.BlockSpec((tm, D), lambda i: (i, 0)),
                  pl.BlockSpec((1, D), lambda i: (0, 0))],
        out_specs=pl.BlockSpec((tm, D), lambda i: (i, 0)),
        out_shape=jax.ShapeDtypeStruct((M, D), BF16),
        compiler_params=_cparams(("parallel",)),
        name="first_norm",
    )(x, g.reshape(1, D))


def _residual_body(h_ref, y_ref, gp_ref, gn_ref, ho_ref, u_ref):
    hn = h_ref[...] + _rms(y_ref[...], gp_ref[...])
    ho_ref[...] = hn
    u_ref[...] = _rms(hn, gn_ref[...]).astype(BF16)


def residual_norm(h, y, g_post, g_next):
    M, D = h.shape
    tm = _tile(M, 256)
    row = pl.BlockSpec((tm, D), lambda i: (i, 0))
    vec = pl.BlockSpec((1, D), lambda i: (0, 0))
    return pl.pallas_call(
        _residual_body,
        grid=(M // tm,),
        in_specs=[row, row, vec, vec],
        out_specs=[row, row],
        out_shape=[jax.ShapeDtypeStruct((M, D), F32), jax.ShapeDtypeStruct((M, D), BF16)],
        compiler_params=_cparams(("parallel",)),
        name="residual_norm",
    )(h, y, g_post.reshape(1, D), g_next.reshape(1, D))


def _ple_body(t_ref, p_ref, wu_ref, wp_ref, h_ref, gn_ref, ho_ref, *maybe_u_ref):
    gate = _sigmoid(_dot(t_ref[...], wu_ref[...]))
    inj = _dot(p_ref[...].astype(BF16), wp_ref[...])
    hn = h_ref[...] + gate * inj
    ho_ref[...] = hn
    if maybe_u_ref:
        maybe_u_ref[0][...] = _rms(hn, gn_ref[...]).astype(BF16)


def ple_inject(t, p, w_up, w_proj, h, g_next):
    M, D = h.shape
    R = t.shape[1]
    P = p.shape[1]
    tm = _tile(M, 256)
    row = pl.BlockSpec((tm, D), lambda i: (i, 0))
    vec = pl.BlockSpec((1, D), lambda i: (0, 0))
    want_u = g_next is not None
    out_specs = [row, row] if want_u else [row]
    out_shape = [jax.ShapeDtypeStruct((M, D), F32)]
    if want_u:
        out_shape.append(jax.ShapeDtypeStruct((M, D), BF16))
    gn = g_next if want_u else jnp.ones((D,), F32)
    res = pl.pallas_call(
        _ple_body,
        grid=(M // tm,),
        in_specs=[pl.BlockSpec((tm, R), lambda i: (i, 0)),
                  pl.BlockSpec((tm, P), lambda i: (i, 0)),
                  pl.BlockSpec((R, D), lambda i: (0, 0)),
                  pl.BlockSpec((P, D), lambda i: (0, 0)),
                  row, vec],
        out_specs=out_specs,
        out_shape=out_shape,
        compiler_params=_cparams(("parallel",)),
        name="ple_inject",
    )(t, p, w_up, w_proj, h, gn.reshape(1, D))
    return (res[0], res[1]) if want_u else (res[0], None)


def _matmul_body(*refs, n_pairs, scaled):
    o_ref = refs[-1]
    for cols in _col_subtiles(o_ref.shape[1]):
        acc = _dot(refs[0][...], refs[1][:, cols])
        for q in range(1, n_pairs):
            acc = acc + _dot(refs[2 * q][...], refs[2 * q + 1][:, cols])
        if scaled:
            acc = acc * refs[2 * n_pairs][:, cols]
        o_ref[:, cols] = acc.astype(o_ref.dtype)


def matmul(pairs, out_dtype, colscale=None, name="matmul"):
    M = pairs[0][0].shape[0]
    N = pairs[0][1].shape[1]
    tm = _tile(M, ROW_TILE)
    tn = _tile(N, COL_TILE)
    in_specs, args = [], []
    for x, w in pairs:
        K = x.shape[1]
        in_specs += [pl.BlockSpec((tm, K), lambda i, j: (i, 0)),
                     pl.BlockSpec((K, tn), lambda i, j: (0, j))]
        args += [x, w]
    if colscale is not None:
        in_specs.append(pl.BlockSpec((1, tn), lambda i, j: (0, j)))
        args.append(colscale)
    return pl.pallas_call(
        functools.partial(_matmul_body, n_pairs=len(pairs), scaled=colscale is not None),
        grid=(M // tm, N // tn),
        in_specs=in_specs,
        out_specs=pl.BlockSpec((tm, tn), lambda i, j: (i, j)),
        out_shape=jax.ShapeDtypeStruct((M, N), out_dtype),
        compiler_params=_cparams(("parallel", "arbitrary")),
        name=name,
    )(*args)


def _rope_proj_body(u_ref, w_ref, cs_ref, cos_ref, sin_ref, o_ref):
    cos = cos_ref[...]
    sin = sin_ref[...]
    u = u_ref[...]
    for cols in _col_subtiles(o_ref.shape[1]):
        a = _dot(u, w_ref[:, cols]) * cs_ref[:, cols]
        for hh in range((cols.stop - cols.start) // HEAD_DIM):
            x = a[:, hh * HEAD_DIM:(hh + 1) * HEAD_DIM]
            lo = cols.start + hh * HEAD_DIM
            o_ref[:, lo:lo + HEAD_DIM] = (x * cos + pltpu.roll(x, HEAD_DIM // 2, 1) * sin).astype(o_ref.dtype)


def rope_proj(u, w, colscale, cosf, sins):
    M, K = u.shape
    N = w.shape[1]
    tm = _tile(M, ROW_TILE)
    tn = _tile(N, COL_TILE)
    return pl.pallas_call(
        _rope_proj_body,
        grid=(M // tm, N // tn),
        in_specs=[pl.BlockSpec((tm, K), lambda i, j: (i, 0)),
                  pl.BlockSpec((K, tn), lambda i, j: (0, j)),
                  pl.BlockSpec((1, tn), lambda i, j: (0, j)),
                  pl.BlockSpec((tm, HEAD_DIM), lambda i, j: (i, 0)),
                  pl.BlockSpec((tm, HEAD_DIM), lambda i, j: (i, 0))],
        out_specs=pl.BlockSpec((tm, tn), lambda i, j: (i, j)),
        out_shape=jax.ShapeDtypeStruct((M, N), BF16),
        compiler_params=_cparams(("parallel", "arbitrary")),
        name="rope_proj",
    )(u, w, colscale, cosf, sins)


def _row_pieces(tm):
    rp = min(tm, ROW_PIECE)
    assert tm % rp == 0
    return [slice(r, r + rp) for r in range(0, tm, rp)]


def _conv_rows(xs_ref, cols, rows, w, b):
    K = w.shape[0]
    lo, hi = SUBLANES + rows.start, SUBLANES + rows.stop
    y = b + w[K - 1:K, :] * xs_ref[lo:hi, cols]
    for d in range(1, K):
        y = y + w[K - 1 - d:K - d, :] * xs_ref[lo - d:hi - d, cols]
    return y


def _staged_conv_pipeline(tm, tn, streams, epilogue):
    row_pieces = _row_pieces(tm)
    blocks = [(cols, rows) for cols in _col_subtiles(tn) for rows in row_pieces]

    def stage(cols, rows):
        for u_ref, w_ref, xs_ref, carry_ref in streams:
            if rows.start == 0:
                xs_ref[0:SUBLANES, cols] = carry_ref[:, cols]
            xs_ref[SUBLANES + rows.start:SUBLANES + rows.stop, cols] = _dot(u_ref[rows, :], w_ref[:, cols])
            if rows.stop == tm:
                carry_ref[:, cols] = xs_ref[tm:tm + SUBLANES, cols]

    stage(*blocks[0])
    for idx, (cols, rows) in enumerate(blocks):
        if idx + 1 < len(blocks):
            stage(*blocks[idx + 1])
        epilogue(cols, rows)


def _conv_proj_body(u_ref, w_ref, cs_ref, cw_ref, cb_ref, o_ref, carry_ref, xs_ref):
    @pl.when(pl.program_id(0) == 0)
    def _():
        carry_ref[...] = jnp.zeros_like(carry_ref)

    def epilogue(cols, rows):
        y = _conv_rows(xs_ref, cols, rows, cw_ref[:, cols], cb_ref[:, cols])
        o_ref[rows, cols] = (y * _sigmoid(y) * cs_ref[:, cols]).astype(o_ref.dtype)

    _staged_conv_pipeline(o_ref.shape[0], o_ref.shape[1], [(u_ref, w_ref, xs_ref, carry_ref)], epilogue)


def conv_proj(u, w, colscale, conv_w, conv_b):
    M, K = u.shape
    N = w.shape[1]
    tm = _tile(M, ROW_TILE)
    kc = conv_w.shape[0]
    return pl.pallas_call(
        _conv_proj_body,
        grid=(M // tm,),
        in_specs=[pl.BlockSpec((tm, K), lambda i: (i, 0)),
                  pl.BlockSpec((K, N), lambda i: (0, 0)),
                  pl.BlockSpec((1, N), lambda i: (0, 0)),
                  pl.BlockSpec((kc, N), lambda i: (0, 0)),
                  pl.BlockSpec((1, N), lambda i: (0, 0))],
        out_specs=pl.BlockSpec((tm, N), lambda i: (i, 0)),
        out_shape=jax.ShapeDtypeStruct((M, N), BF16),
        scratch_shapes=[pltpu.VMEM((SUBLANES, N), F32), pltpu.VMEM((tm + SUBLANES, N), F32)],
        compiler_params=_cparams(("arbitrary",)),
        name="conv_proj",
    )(u, w, colscale, conv_w, conv_b.reshape(1, N))


def _gate_proj_body(u_ref, w_ref, wt_ref, o_ref, ot_ref):
    u = u_ref[...]
    o_ref[...] = _dot(u, w_ref[...])
    ot_ref[...] = _dot_nt(wt_ref[...], u)


def gate_proj(u, w_pad, wt_pad):
    M, K = u.shape
    tm = _tile(M, 512)
    return pl.pallas_call(
        _gate_proj_body,
        grid=(M // tm,),
        in_specs=[pl.BlockSpec((tm, K), lambda i: (i, 0)),
                  pl.BlockSpec((K, LANES), lambda i: (0, 0)),
                  pl.BlockSpec((SUBLANES, K), lambda i: (0, 0))],
        out_specs=[pl.BlockSpec((tm, LANES), lambda i: (i, 0)),
                   pl.BlockSpec((SUBLANES, tm), lambda i: (0, i))],
        out_shape=[jax.ShapeDtypeStruct((M, LANES), F32), jax.ShapeDtypeStruct((SUBLANES, M), F32)],
        compiler_params=_cparams(("parallel",)),
        name="gate_proj",
    )(u, w_pad, wt_pad)


def _gelu_tanh(x):
    return 0.5 * x * (1.0 + jnp.tanh(math.sqrt(2.0 / math.pi) * (x + 0.044715 * (x * x * x))))


def _ffn_up_body(u_ref, wa_ref, wb_ref, cwa_ref, cba_ref, cwb_ref, cbb_ref, o_ref,
                 carry_a, carry_b, xs_a, xs_b):
    j = pl.program_id(1)

    @pl.when(pl.program_id(0) == 0)
    def _():
        carry_a[j] = jnp.zeros(carry_a.shape[1:], F32)
        carry_b[j] = jnp.zeros(carry_b.shape[1:], F32)

    def epilogue(cols, rows):
        a = _conv_rows(xs_a, cols, rows, cwa_ref[:, cols], cba_ref[:, cols])
        b = _conv_rows(xs_b, cols, rows, cwb_ref[:, cols], cbb_ref[:, cols])
        o_ref[rows, cols] = (_gelu_tanh(a) * b).astype(o_ref.dtype)

    _staged_conv_pipeline(o_ref.shape[0], o_ref.shape[1],
                          [(u_ref, wa_ref, xs_a, carry_a.at[j]), (u_ref, wb_ref, xs_b, carry_b.at[j])], epilogue)


def ffn_up(u, w_up, conv_w, conv_b):
    M, K = u.shape
    F = w_up.shape[1] // 2
    tm = _tile(M, ROW_TILE)
    tn = _tile(F, COL_TILE // 2)
    nj = F // tn
    kc = conv_w.shape[0]
    cb = conv_b.reshape(1, 2 * F)
    return pl.pallas_call(
        _ffn_up_body,
        grid=(M // tm, nj),
        in_specs=[pl.BlockSpec((tm, K), lambda i, j: (i, 0)),
                  pl.BlockSpec((K, tn), lambda i, j: (0, j)),
                  pl.BlockSpec((K, tn), lambda i, j: (0, j + nj)),
                  pl.BlockSpec((kc, tn), lambda i, j: (0, j)),
                  pl.BlockSpec((1, tn), lambda i, j: (0, j)),
                  pl.BlockSpec((kc, tn), lambda i, j: (0, j + nj)),
                  pl.BlockSpec((1, tn), lambda i, j: (0, j + nj))],
        out_specs=pl.BlockSpec((tm, tn), lambda i, j: (i, j)),
        out_shape=jax.ShapeDtypeStruct((M, F), BF16),
        scratch_shapes=[pltpu.VMEM((nj, SUBLANES, tn), F32), pltpu.VMEM((nj, SUBLANES, tn), F32),
                        pltpu.VMEM((tm + SUBLANES, tn), F32), pltpu.VMEM((tm + SUBLANES, tn), F32)],
        compiler_params=_cparams(("arbitrary", "arbitrary")),
        name="ffn_up",
    )(u, w_up, w_up, conv_w, cb, conv_w, cb)


def _diff_attn_body(lam_ref, q_ref, k_ref, v_ref, g_ref, o_ref, acc_ref, l_ref, knorm_ref, *, tq, lam_init):
    qi = pl.program_id(1)
    n_kv = k_ref.shape[0] // tq
    q = q_ref[...]
    qs = (q[:, :HEAD_DIM], q[:, HEAD_DIM:])
    ksl = (slice(0, HEAD_DIM), slice(HEAD_DIM, 2 * HEAD_DIM))

    @pl.when(qi == 0)
    def _():
        def body(c, carry):
            kk = k_ref[pl.ds(pl.multiple_of(c * tq, tq), tq), :].astype(F32)
            sq = kk * kk
            return tuple(jnp.maximum(carry[mp], jnp.max(jnp.sum(sq[:, ksl[mp]], axis=-1, keepdims=True),
                                                        axis=0, keepdims=True)) for mp in range(2))
        zero = jnp.zeros((1, 1), F32)
        norms = lax.fori_loop(0, n_kv, body, (zero, zero))
        for mp in range(2):
            knorm_ref[mp] = jnp.broadcast_to(norms[mp], (1, LANES))

    row = lax.broadcasted_iota(jnp.int32, (tq, tq), 0) // DIFF_CHUNK
    col = lax.broadcasted_iota(jnp.int32, (tq, tq), 1) // DIFF_CHUNK
    visible = col <= row
    start = pl.multiple_of(qi * tq, tq)

    def lane_partial(p):
        out = p[:, 0:LANES]
        for c in range(1, p.shape[1] // LANES):
            out = out + p[:, c * LANES:(c + 1) * LANES]
        return out

    bounds = []
    for mp in range(2):
        qf = qs[mp].astype(F32)
        qn2 = jnp.sum(qf * qf, axis=-1, keepdims=True)
        bounds.append(jnp.sqrt(qn2 * knorm_ref[mp][:, 0:1]))
    kd = k_ref[pl.ds(start, tq), :]
    vd = v_ref[pl.ds(start, tq), :]
    for mp in range(2):
        p = jnp.where(visible, jnp.exp2(_dot_nt(qs[mp], kd[:, ksl[mp]]) - bounds[mp]), 0.0)
        l_ref[mp] = lane_partial(p)
        acc_ref[mp] = _dot(p.astype(BF16), vd)

    def fast_step(kb, carry):
        off = pl.multiple_of(kb * tq, tq)
        kk = k_ref[pl.ds(off, tq), :]
        vv = v_ref[pl.ds(off, tq), :]
        for mp in range(2):
            p = jnp.exp2(_dot_nt(qs[mp], kk[:, ksl[mp]]) - bounds[mp])
            l_ref[mp] += lane_partial(p)
            acc_ref[mp] += _dot(p.astype(BF16), vv)
        return carry

    lax.fori_loop(0, qi, fast_step, 0)

    l_min = jnp.minimum(jnp.min(jnp.sum(l_ref[0], axis=-1, keepdims=True)),
                        jnp.min(jnp.sum(l_ref[1], axis=-1, keepdims=True)))
    redo = jnp.logical_not(l_min >= DIFF_MIN_NORMALISER)

    @pl.when(redo)
    def _():
        lane0 = lax.broadcasted_iota(jnp.int32, (tq, LANES), 1) == 0
        init = []
        for mp in range(2):
            s = jnp.where(visible, _dot_nt(qs[mp], kd[:, ksl[mp]]), NEG)
            m = jnp.max(s, axis=-1, keepdims=True)
            p = jnp.exp2(s - m)
            acc_ref[mp] = _dot(p.astype(BF16), vd)
            init += [m, jnp.sum(p, axis=-1, keepdims=True)]

        def step(kb, carry):
            off = pl.multiple_of(kb * tq, tq)
            kk = k_ref[pl.ds(off, tq), :]
            vv = v_ref[pl.ds(off, tq), :]
            out = []
            for mp in range(2):
                m, l = carry[2 * mp], carry[2 * mp + 1]
                s = _dot_nt(qs[mp], kk[:, ksl[mp]])
                m_new = jnp.maximum(m, jnp.max(s, axis=-1, keepdims=True))
                alpha = jnp.exp2(m - m_new)
                p = jnp.exp2(s - m_new)
                acc_ref[mp] = alpha * acc_ref[mp] + _dot(p.astype(BF16), vv)
                out += [m_new, alpha * l + jnp.sum(p, axis=-1, keepdims=True)]
            return tuple(out)

        fin = lax.fori_loop(0, qi, step, tuple(init))
        for mp in range(2):
            l_ref[mp] = jnp.where(lane0, fin[2 * mp + 1], 0.0)

    lp = lam_ref[...]
    lam = (jnp.exp(jnp.sum(lp[0:1, :] * lp[1:2, :], axis=-1, keepdims=True))
           - jnp.exp(jnp.sum(lp[2:3, :] * lp[3:4, :], axis=-1, keepdims=True)) + lam_init)
    l0 = jnp.sum(l_ref[0], axis=-1, keepdims=True)
    l1 = jnp.sum(l_ref[1], axis=-1, keepdims=True)
    out = acc_ref[0] / l0 - lam * (acc_ref[1] / l1)
    o_ref[...] = (_rms(out, g_ref[...]) * (1.0 - lam_init)).astype(o_ref.dtype)


def diff_attention(q, k, v, lam_p, g_sub, lam_init, n_heads):
    S = q[0].shape[0]
    W = 2 * HEAD_DIM
    tq = _tile(S, 512)
    return pl.pallas_call(
        functools.partial(_diff_attn_body, tq=tq, lam_init=lam_init),
        grid=(n_heads, S // tq),
        in_specs=[pl.BlockSpec((4, HEAD_DIM), lambda h, i: (0, 0)),
                  pl.BlockSpec((tq, W), lambda h, i: (i, q[1] // W + h)),
                  pl.BlockSpec((S, W), lambda h, i: (0, k[1] // W + h)),
                  pl.BlockSpec((S, W), lambda h, i: (0, v[1] // W + h)),
                  pl.BlockSpec((1, W), lambda h, i: (0, 0))],
        out_specs=pl.BlockSpec((tq, W), lambda h, i: (i, h)),
        out_shape=jax.ShapeDtypeStruct((S, n_heads * W), BF16),
        scratch_shapes=[pltpu.VMEM((2, tq, W), F32), pltpu.VMEM((2, tq, LANES), F32),
                        pltpu.VMEM((2, 1, LANES), F32)],
        compiler_params=_cparams(("parallel", "arbitrary")),
        name="diff_attention",
    )(lam_p, q[0], k[0], v[0], g_sub.reshape(1, W))


def _stick_breaking_body(q_ref, k_ref, v_ref, o_ref, acc_ref, *, tq, heads):
    qi = pl.program_id(1)
    row = lax.broadcasted_iota(jnp.int32, (tq, tq), 0)
    col = lax.broadcasted_iota(jnp.int32, (tq, tq), 1)
    strict = col < row
    suffix_ones = (row >= col).astype(BF16)

    def scores(hh, kb, diag):
        sl = slice(hh * HEAD_DIM, (hh + 1) * HEAD_DIM)
        off = pl.multiple_of(kb * tq, tq)
        z = _dot_nt(q_ref[:, sl], k_ref[pl.ds(off, tq), sl])
        log_keep = -(jnp.maximum(z, 0.0) + jnp.log1p(jnp.exp(-jnp.abs(z))))
        if diag:
            log_keep = jnp.where(strict, log_keep, 0.0)
        hi = log_keep.astype(BF16)
        lo = (log_keep - hi.astype(F32)).astype(BF16)
        cum = _dot(hi, suffix_ones) + _dot(lo, suffix_ones)
        return z + cum, cum[:, 0:1], v_ref[pl.ds(off, tq), sl]

    has_prev = qi > 0
    prev = jnp.maximum(qi - 1, 0)
    runs = []
    for hh in range(heads):
        zc_d, tot_d, v_d = scores(hh, qi, True)
        zc_p, tot_p, v_p = scores(hh, prev, False)
        a_d = jnp.where(strict, jnp.exp(zc_d), 0.0)
        a_p = jnp.where(has_prev, jnp.exp(zc_p + tot_d), 0.0)
        acc_ref[hh] = _dot(a_d.astype(BF16), v_d) + _dot(a_p.astype(BF16), v_p)
        runs.append(tot_d + jnp.where(has_prev, tot_p, 0.0))

    def needs_more(rs):
        return functools.reduce(jnp.maximum, [jnp.max(r) for r in rs]) > SB_LOG_FLOOR

    def cond(state):
        return jnp.logical_and(state[0] >= 0, state[1])

    def step(state):
        kb = state[0]
        new_runs = []
        for hh in range(heads):
            run = state[2 + hh]
            zc, tot, vv = scores(hh, kb, False)
            acc_ref[hh] += _dot(jnp.exp(zc + run).astype(BF16), vv)
            new_runs.append(run + tot)
        return (kb - 1, needs_more(new_runs), *new_runs)

    lax.while_loop(cond, step, (qi - 2, needs_more(runs), *runs))
    for hh in range(heads):
        o_ref[:, hh * HEAD_DIM:(hh + 1) * HEAD_DIM] = acc_ref[hh].astype(o_ref.dtype)


def stick_breaking(q, k, v, n_heads):
    S = q[0].shape[0]
    heads = 2
    W = heads * HEAD_DIM
    tq = _tile(S, 256)
    return pl.pallas_call(
        functools.partial(_stick_breaking_body, tq=tq, heads=heads),
        grid=(n_heads // heads, S // tq),
        in_specs=[pl.BlockSpec((tq, W), lambda h, i: (i, q[1] // W + h)),
                  pl.BlockSpec((S, W), lambda h, i: (0, k[1] // W + h)),
                  pl.BlockSpec((S, W), lambda h, i: (0, v[1] // W + h))],
        out_specs=pl.BlockSpec((tq, W), lambda h, i: (i, h)),
        out_shape=jax.ShapeDtypeStruct((S, n_heads * HEAD_DIM), BF16),
        scratch_shapes=[pltpu.VMEM((heads, tq, HEAD_DIM), F32)],
        compiler_params=_cparams(("parallel", "parallel")),
        name="stick_breaking",
    )(q[0], k[0], v[0])


def _hi_lo(x):
    hi = x.astype(BF16)
    return hi, (x - hi.astype(F32)).astype(BF16)


def _mlstm_body(q_ref, k_ref, v_ref, og_ref, gc_ref, gr_ref, bc_ref, br_ref, gn_ref, o_ref,
                c_ref, n_ref, m_ref, *, n_heads, L):
    dv = 2 * HEAD_DIM

    @pl.when(pl.program_id(0) == 0)
    def _():
        c_ref[...] = jnp.zeros_like(c_ref)
        n_ref[...] = jnp.zeros_like(n_ref)
        m_ref[...] = jnp.zeros_like(m_ref)

    row = lax.broadcasted_iota(jnp.int32, (L, L), 0)
    col = lax.broadcasted_iota(jnp.int32, (L, L), 1)
    tril = col <= row
    tril_ones = tril.astype(BF16)
    triu_ones = (row <= col).astype(BF16)

    gc = gc_ref[...] + bc_ref[...]
    gr = gr_ref[...] + br_ref[...]
    hi, lo = _hi_lo(_log_sigmoid(gc))
    cum_c = _dot(tril_ones, hi) + _dot(tril_ones, lo)
    hi, lo = _hi_lo(_log_sigmoid(gr))
    cum_r = _dot(hi, triu_ones) + _dot(lo, triu_ones)

    gnorm = gn_ref[...]
    for h in range(n_heads):
        b_c = cum_c[:, n_heads + h:n_heads + h + 1]
        li_c = gc[:, h:h + 1]
        b_r = cum_r[n_heads + h:n_heads + h + 1, :]
        li_r = gr[h:h + 1, :]
        m_prev = m_ref[:, h:h + 1]

        dm = jnp.where(tril, b_c - b_r + li_r, NEG)
        inter = b_c + m_prev
        m_t = jnp.maximum(inter, jnp.max(dm, axis=-1, keepdims=True))
        w_intra = jnp.exp(dm - m_t)
        w_inter = jnp.exp(inter - m_t)

        qh = q_ref[:, h * HEAD_DIM:(h + 1) * HEAD_DIM]
        kh = k_ref[:, h * HEAD_DIM:(h + 1) * HEAD_DIM]
        vh = v_ref[:, h * dv:(h + 1) * dv]
        s = _dot_nt(qh, kh) * w_intra
        num = _dot(s.astype(BF16), vh) + w_inter * _dot(qh, c_ref[h].astype(BF16))
        nq = (jnp.sum(s, axis=-1, keepdims=True)
              + w_inter * jnp.sum(qh.astype(F32) * n_ref[h], axis=-1, keepdims=True))
        den = jnp.maximum(jnp.abs(nq), jnp.exp(-m_t))
        hh = num / den

        bl = b_c[L - 1:L, :]
        g_c = bl - b_c + li_c
        m_new = jnp.maximum(bl + m_prev, jnp.max(g_c, axis=0, keepdims=True))
        wk = jnp.exp(g_c - m_new)
        dec = jnp.exp(bl + m_prev - m_new)
        kw = kh.astype(F32) * wk
        c_ref[h] = dec * c_ref[h] + _dot(kw.T.astype(BF16), vh)
        n_ref[h] = dec * n_ref[h] + jnp.sum(kw, axis=0, keepdims=True)
        m_ref[:, h:h + 1] = m_new

        og = og_ref[:, h * dv:(h + 1) * dv].astype(F32)
        o_ref[:, h * dv:(h + 1) * dv] = (_rms(hh, gnorm) * _sigmoid(og)).astype(o_ref.dtype)


def mlstm(q, k, v, og, gates_c, gates_r, bias_c, bias_r, g_norm, n_heads):
    S = q[0].shape[0]
    L = _tile(S, REC_CHUNK)
    qk_w = n_heads * HEAD_DIM
    v_w = n_heads * 2 * HEAD_DIM
    return pl.pallas_call(
        functools.partial(_mlstm_body, n_heads=n_heads, L=L),
        grid=(S // L,),
        in_specs=[pl.BlockSpec((L, qk_w), lambda c: (c, q[1] // qk_w)),
                  pl.BlockSpec((L, qk_w), lambda c: (c, k[1] // qk_w)),
                  pl.BlockSpec((L, v_w), lambda c: (c, v[1] // v_w)),
                  pl.BlockSpec((L, v_w), lambda c: (c, og[1] // v_w)),
                  pl.BlockSpec((L, LANES), lambda c: (c, 0)),
                  pl.BlockSpec((SUBLANES, L), lambda c: (0, c)),
                  pl.BlockSpec((1, LANES), lambda c: (0, 0)),
                  pl.BlockSpec((SUBLANES, 1), lambda c: (0, 0)),
                  pl.BlockSpec((1, 2 * HEAD_DIM), lambda c: (0, 0))],
        out_specs=pl.BlockSpec((L, v_w), lambda c: (c, 0)),
        out_shape=jax.ShapeDtypeStruct((S, v_w), BF16),
        scratch_shapes=[pltpu.VMEM((n_heads, HEAD_DIM, 2 * HEAD_DIM), F32),
                        pltpu.VMEM((n_heads, 1, HEAD_DIM), F32),
                        pltpu.VMEM((1, LANES), F32)],
        compiler_params=_cparams(("arbitrary",)),
        name="mlstm",
    )(q[0], k[0], v[0], og[0], gates_c, gates_r, bias_c, bias_r, g_norm.reshape(1, 2 * HEAD_DIM))


def _retention_body(q_ref, k_ref, v_ref, cg_ref, gn_ref, o_ref, r_ref, dmask_ref, *, n_heads, L):
    dv = 2 * HEAD_DIM
    log_g = [math.log(1.0 - 2.0 ** (-5.0 - h)) for h in range(n_heads)]

    @pl.when(pl.program_id(0) == 0)
    def _():
        r_ref[...] = jnp.zeros_like(r_ref)
        diff = (lax.broadcasted_iota(jnp.int32, (L, L), 0)
                - lax.broadcasted_iota(jnp.int32, (L, L), 1)).astype(F32)
        for h in range(n_heads):
            dmask_ref[h] = jnp.where(diff >= 0, jnp.exp(jnp.maximum(diff, 0.0) * log_g[h]), 0.0)

    idx = lax.broadcasted_iota(jnp.int32, (L, 1), 0).astype(F32)
    gnorm = gn_ref[...]
    for h in range(n_heads):
        q_dec = jnp.exp((idx + 1.0) * log_g[h])
        k_dec = jnp.exp((L - 1.0 - idx) * log_g[h])
        c_dec = math.exp(L * log_g[h])
        qh = q_ref[:, h * HEAD_DIM:(h + 1) * HEAD_DIM]
        kh = k_ref[:, h * HEAD_DIM:(h + 1) * HEAD_DIM]
        vh = v_ref[:, h * dv:(h + 1) * dv]
        s = _dot_nt(qh, kh) * dmask_ref[h]
        o = _dot(s.astype(BF16), vh) + q_dec * _dot(qh, r_ref[h].astype(BF16))
        kd = kh.astype(F32) * k_dec
        r_ref[h] = c_dec * r_ref[h] + _dot(kd.T.astype(BF16), vh)
        cg = cg_ref[:, h * dv:(h + 1) * dv].astype(F32)
        o_ref[:, h * dv:(h + 1) * dv] = (_rms(o, gnorm) * (cg * _sigmoid(cg))).astype(o_ref.dtype)


def retention(q, k, v, cg, g_norm, n_heads):
    S = q[0].shape[0]
    L = _tile(S, REC_CHUNK)
    qk_w = n_heads * HEAD_DIM
    v_w = n_heads * 2 * HEAD_DIM
    return pl.pallas_call(
        functools.partial(_retention_body, n_heads=n_heads, L=L),
        grid=(S // L,),
        in_specs=[pl.BlockSpec((L, qk_w), lambda c: (c, q[1] // qk_w)),
                  pl.BlockSpec((L, qk_w), lambda c: (c, k[1] // qk_w)),
                  pl.BlockSpec((L, v_w), lambda c: (c, v[1] // v_w)),
                  pl.BlockSpec((L, v_w), lambda c: (c, cg[1] // v_w)),
                  pl.BlockSpec((1, 2 * HEAD_DIM), lambda c: (0, 0))],
        out_specs=pl.BlockSpec((L, v_w), lambda c: (c, 0)),
        out_shape=jax.ShapeDtypeStruct((S, v_w), BF16),
        scratch_shapes=[pltpu.VMEM((n_heads, HEAD_DIM, 2 * HEAD_DIM), F32),
                        pltpu.VMEM((n_heads, L, L), F32)],
        compiler_params=_cparams(("arbitrary",)),
        name="retention",
    )(q[0], k[0], v[0], cg[0], g_norm.reshape(1, 2 * HEAD_DIM))


def _rope_tables(S):
    inv = ROPE_THETA ** (-jnp.arange(0, HEAD_DIM, 2, dtype=F32) / HEAD_DIM)
    ang = jnp.arange(S, dtype=F32)[:, None] * inv[None, :]
    cos, sin = jnp.cos(ang), jnp.sin(ang)
    return jnp.concatenate([cos, cos], axis=-1), jnp.concatenate([-sin, sin], axis=-1)


def _col_scale(n, scaled_ranges):
    cs = jnp.ones((1, n), F32)
    for lo, hi, val in scaled_ranges:
        cs = cs.at[:, lo:hi].set(val)
    return cs


def kernel(x, p, w_in_even, w_out_even, diff_lambda, g_diff_sub, w_conv_qk, b_conv_qk, b_igate, b_fgate, g_mlstm_head, w_in_odd, w_out_odd, g_ret_head, g_mix_pre, g_mix_post, g_ffn_pre, g_ffn_post, w_ffn_up, w_ffn_conv, b_ffn_conv, w_ffn_down, g_ple, w_ple_gate_down, w_ple_gate_up, w_ple_proj):
    B, S, D = x.shape
    assert B == 1
    depth = g_mix_pre.shape[0]
    n4 = D // 1024
    n_sb = D // 512
    qk4 = n4 * HEAD_DIM
    v4 = n4 * 2 * HEAD_DIM
    assert 2 * qk4 == v4 == n_sb * HEAD_DIM
    inv_sqrt_d = HEAD_DIM ** -0.5

    cosf, sins = _rope_tables(S)
    cs_even_rope = _col_scale(2 * v4, [(0, v4, inv_sqrt_d * LOG2E)])
    cs_even_conv = _col_scale(v4, [(0, qk4, inv_sqrt_d)])
    cs_odd_rope = _col_scale(v4, [(qk4, 2 * qk4, inv_sqrt_d)])
    cs_odd_plain = _col_scale(5 * v4, [(2 * v4, 3 * v4, inv_sqrt_d)])

    h = x.reshape(S, D)
    u = first_norm(h, g_mix_pre[0])
    for i in range(depth):
        j = i // 2
        if i % 2 == 0:
            lam_init = 0.8 - 0.6 * math.exp(-0.3 * i)
            w_in = w_in_even[j]
            qk_a = rope_proj(u, w_in[:, :2 * v4].astype(BF16), cs_even_rope, cosf, sins)
            w_plain = jnp.concatenate([w_in[:, 2 * v4:3 * v4], w_in[:, 4 * v4:6 * v4]], axis=1)
            plain = matmul([(u, w_plain.astype(BF16))], BF16, name="in_proj_plain")
            qk_b = conv_proj(u, w_in[:, 3 * v4:4 * v4].astype(BF16), cs_even_conv,
                             w_conv_qk[j], b_conv_qk[j])
            w_gate = w_in[:, 6 * v4:]
            w_gate_c = jnp.pad(w_gate, ((0, 0), (0, LANES - 2 * n4))).astype(BF16)
            w_gate_r = jnp.pad(w_gate.T, ((0, SUBLANES - 2 * n4), (0, 0))).astype(BF16)
            gates_c, gates_r = gate_proj(u, w_gate_c, w_gate_r)
            bias = jnp.concatenate([b_igate[j], b_fgate[j]])
            bias_c = jnp.pad(bias, (0, LANES - 2 * n4)).reshape(1, LANES)
            bias_r = jnp.pad(bias, (0, SUBLANES - 2 * n4)).reshape(SUBLANES, 1)
            ya = diff_attention((qk_a, 0), (qk_a, v4), (plain, 0), diff_lambda[j], g_diff_sub[j], lam_init, n4)
            yb = mlstm((qk_b, 0), (qk_b, qk4), (plain, v4), (plain, 2 * v4),
                       gates_c, gates_r, bias_c, bias_r, g_mlstm_head[j], n4)
            w_out = w_out_even[j].astype(BF16)
        else:
            w_in = w_in_odd[j]
            qk_c = rope_proj(u, w_in[:, :v4].astype(BF16), cs_odd_rope, cosf, sins)
            plain = matmul([(u, w_in[:, v4:].astype(BF16))], BF16, colscale=cs_odd_plain,
                           name="in_proj_plain")
            ya = retention((qk_c, 0), (qk_c, qk4), (plain, 0), (plain, v4), g_ret_head[j], n4)
            yb = stick_breaking((plain, 2 * v4), (plain, 3 * v4), (plain, 4 * v4), n_sb)
            w_out = w_out_odd[j].astype(BF16)
        mix = matmul([(ya, w_out[:v4]), (yb, w_out[v4:])], F32, name="out_proj")
        h, u = residual_norm(h, mix, g_mix_post[i], g_ffn_pre[i])
        hid = ffn_up(u, w_ffn_up[i].astype(BF16), w_ffn_conv[i], b_ffn_conv[i])
        f = matmul([(hid, w_ffn_down[i].astype(BF16))], F32, name="ffn_down")
        h, u = residual_norm(h, f, g_ffn_post[i], g_ple[i])
        t = matmul([(u, w_ple_gate_down[i].astype(BF16))], BF16, name="ple_down")
        g_next = g_mix_pre[i + 1] if i + 1 < depth else None
        h, u = ple_inject(t, p[i, 0], w_ple_gate_up[i].astype(BF16), w_ple_proj[i].astype(BF16), h, g_next)
    return h.reshape(B, S, D)
```

```python
import functools
import math

import jax
import jax.numpy as jnp
from jax import lax
from jax.experimental import pallas as pl
from jax.experimental.pallas import tpu as pltpu

F32 = jnp.float32
BF16 = jnp.bfloat16

HEAD_DIM = 128
ROPE_THETA = 10000.0
EPS = 1e-6
DIFF_CHUNK = 64
DIFF_UNROLL = 4
REC_CHUNK = 256
NEG = -1e30
LOG2E = math.log2(math.e)
DIFF_MIN_NORMALISER = 2.0 ** -80
SB_LOG_FLOOR = -120.0
VMEM_LIMIT = 56 * 1024 * 1024
LANES = 128
SUBLANES = 8
MXU_COLS = 256
ROW_TILE = 1024
COL_TILE = 1024
ROW_PIECE = 512


def _cparams(sem):
    return pltpu.CompilerParams(dimension_semantics=sem, vmem_limit_bytes=VMEM_LIMIT)


def _dot(a, b):
    return jnp.dot(a, b, preferred_element_type=F32)


def _dot_nt(a, b):
    return lax.dot_general(a, b, (((1,), (1,)), ((), ())), preferred_element_type=F32)


def _rms(x, g):
    return x * lax.rsqrt(jnp.mean(x * x, axis=-1, keepdims=True) + EPS) * g


def _log_sigmoid(x):
    return jnp.minimum(x, 0.0) - jnp.log1p(jnp.exp(-jnp.abs(x)))


def _sigmoid(x):
    return 1.0 / (1.0 + jnp.exp(-x))


def _tile(n, pref):
    if n <= pref:
        return n
    t = pref - pref % MXU_COLS
    while n % t:
        t -= MXU_COLS
    assert t > 0, (n, pref)
    return t


def _col_subtiles(tn):
    w = min(tn, MXU_COLS)
    assert tn % w == 0
    return [slice(c * w, (c + 1) * w) for c in range(tn // w)]


def _norm_body(x_ref, g_ref, u_ref):
    u_ref[...] = _rms(x_ref[...], g_ref[...]).astype(BF16)


def first_norm(x, g):
    M, D = x.shape
    tm = _tile(M, 256)
    return pl.pallas_call(
        _norm_body,
        grid=(M // tm,),
        in_specs=[pl.BlockSpec((tm, D), lambda i: (i, 0)),
                  pl.BlockSpec((1, D), lambda i: (0, 0))],
        out_specs=pl.BlockSpec((tm, D), lambda i: (i, 0)),
        out_shape=jax.ShapeDtypeStruct((M, D), BF16),
        compiler_params=_cparams(("parallel",)),
        name="first_norm",
    )(x, g.reshape(1, D))


def _residual_body(h_ref, y_ref, gp_ref, gn_ref, ho_ref, u_ref):
    hn = h_ref[...] + _rms(y_ref[...], gp_ref[...])
    ho_ref[...] = hn
    u_ref[...] = _rms(hn, gn_ref[...]).astype(BF16)


def residual_norm(h, y, g_post, g_next):
    M, D = h.shape
    tm = _tile(M, 256)
    row = pl.BlockSpec((tm, D), lambda i: (i, 0))
    vec = pl.BlockSpec((1, D), lambda i: (0, 0))
    return pl.pallas_call(
        _residual_body,
        grid=(M // tm,),
        in_specs=[row, row, vec, vec],
        out_specs=[row, row],
        out_shape=[jax.ShapeDtypeStruct((M, D), F32), jax.ShapeDtypeStruct((M, D), BF16)],
        compiler_params=_cparams(("parallel",)),
        name="residual_norm",
    )(h, y, g_post.reshape(1, D), g_next.reshape(1, D))


def _ffn_residual_ple_body(h_ref, f_ref, p_ref, gp_ref, gple_ref, wd_ref, wu_ref, wp_ref, gn_ref,
                           ho_ref, *maybe_u_ref):
    h2 = h_ref[...] + _rms(f_ref[...], gp_ref[...])
    t = _dot(_rms(h2, gple_ref[...]).astype(BF16), wd_ref[...])
    gate = _sigmoid(_dot(t.astype(BF16), wu_ref[...]))
    h3 = h2 + gate * _dot(p_ref[...].astype(BF16), wp_ref[...])
    ho_ref[...] = h3
    if maybe_u_ref:
        maybe_u_ref[0][...] = _rms(h3, gn_ref[...]).astype(BF16)


def ffn_residual_ple(h, f, p_all, layer, g_post, g_ple, w_down, w_up, w_proj, g_next):
    M, D = h.shape
    R = w_down.shape[1]
    P = p_all.shape[1]
    tm = _tile(M, 256)
    p_block0 = layer * (M // tm)
    row = pl.BlockSpec((tm, D), lambda i: (i, 0))
    vec = pl.BlockSpec((1, D), lambda i: (0, 0))
    want_u = g_next is not None
    out_specs = [row, row] if want_u else [row]
    out_shape = [jax.ShapeDtypeStruct((M, D), F32)]
    if want_u:
        out_shape.append(jax.ShapeDtypeStruct((M, D), BF16))
    gn = g_next if want_u else g_ple
    res = pl.pallas_call(
        _ffn_residual_ple_body,
        grid=(M // tm,),
        in_specs=[row, row,
                  pl.BlockSpec((tm, P), lambda i: (p_block0 + i, 0)),
                  vec, vec,
                  pl.BlockSpec((D, R), lambda i: (0, 0)),
                  pl.BlockSpec((R, D), lambda i: (0, 0)),
                  pl.BlockSpec((P, D), lambda i: (0, 0)),
                  vec],
        out_specs=out_specs,
        out_shape=out_shape,
        compiler_params=_cparams(("parallel",)),
        name="ffn_residual_ple",
    )(h, f, p_all, g_post.reshape(1, D), g_ple.reshape(1, D), w_down, w_up, w_proj, gn.reshape(1, D))
    return (res[0], res[1]) if want_u else (res[0], None)


def _matmul_body(*refs, n_pairs, scaled):
    o_ref = refs[-1]
    for cols in _col_subtiles(o_ref.shape[1]):
        acc = _dot(refs[0][...], refs[1][:, cols])
        for q in range(1, n_pairs):
            acc = acc + _dot(refs[2 * q][...], refs[2 * q + 1][:, cols])
        if scaled:
            acc = acc * refs[2 * n_pairs][:, cols]
        o_ref[:, cols] = acc.astype(o_ref.dtype)


def matmul(pairs, out_dtype, colscale=None, n_out=None, tn=None, w_col_tile=lambda j: j, name="matmul"):
    M = pairs[0][0].shape[0]
    N = pairs[0][1].shape[1] if n_out is None else n_out
    tm = _tile(M, ROW_TILE)
    tn = _tile(N, COL_TILE) if tn is None else tn
    in_specs, args = [], []
    for x, w, r in pairs:
        K = x.shape[1]
        in_specs += [pl.BlockSpec((tm, K), lambda i, j: (i, 0)),
                     pl.BlockSpec((K, tn), lambda i, j, r=r: (r, w_col_tile(j)))]
        args += [x, w]
    if colscale is not None:
        in_specs.append(pl.BlockSpec((1, tn), lambda i, j: (0, j)))
        args.append(colscale)
    return pl.pallas_call(
        functools.partial(_matmul_body, n_pairs=len(pairs), scaled=colscale is not None),
        grid=(M // tm, N // tn),
        in_specs=in_specs,
        out_specs=pl.BlockSpec((tm, tn), lambda i, j: (i, j)),
        out_shape=jax.ShapeDtypeStruct((M, N), out_dtype),
        compiler_params=_cparams(("parallel", "arbitrary")),
        name=name,
    )(*args)


def _rope_proj_body(u_ref, w_ref, cs_ref, cos_ref, sin_ref, o_ref):
    cos = cos_ref[...]
    sin = sin_ref[...]
    u = u_ref[...]
    for cols in _col_subtiles(o_ref.shape[1]):
        a = _dot(u, w_ref[:, cols]) * cs_ref[:, cols]
        for hh in range((cols.stop - cols.start) // HEAD_DIM):
            x = a[:, hh * HEAD_DIM:(hh + 1) * HEAD_DIM]
            lo = cols.start + hh * HEAD_DIM
            o_ref[:, lo:lo + HEAD_DIM] = (x * cos + pltpu.roll(x, HEAD_DIM // 2, 1) * sin).astype(o_ref.dtype)


def rope_proj(u, w, colscale, cosf, sins):
    M, K = u.shape
    N = colscale.shape[1]
    tm = _tile(M, ROW_TILE)
    tn = _tile(N, COL_TILE)
    return pl.pallas_call(
        _rope_proj_body,
        grid=(M // tm, N // tn),
        in_specs=[pl.BlockSpec((tm, K), lambda i, j: (i, 0)),
                  pl.BlockSpec((K, tn), lambda i, j: (0, j)),
                  pl.BlockSpec((1, tn), lambda i, j: (0, j)),
                  pl.BlockSpec((tm, HEAD_DIM), lambda i, j: (i, 0)),
                  pl.BlockSpec((tm, HEAD_DIM), lambda i, j: (i, 0))],
        out_specs=pl.BlockSpec((tm, tn), lambda i, j: (i, j)),
        out_shape=jax.ShapeDtypeStruct((M, N), BF16),
        compiler_params=_cparams(("parallel", "arbitrary")),
        name="rope_proj",
    )(u, w, colscale, cosf, sins)


def _row_pieces(tm):
    rp = min(tm, ROW_PIECE)
    assert tm % rp == 0
    return [slice(r, r + rp) for r in range(0, tm, rp)]


def _conv_rows(xs_ref, cols, rows, w, b):
    K = w.shape[0]
    lo, hi = SUBLANES + rows.start, SUBLANES + rows.stop
    y = b + w[K - 1:K, :] * xs_ref[lo:hi, cols]
    for d in range(1, K):
        y = y + w[K - 1 - d:K - d, :] * xs_ref[lo - d:hi - d, cols]
    return y


def _staged_conv_pipeline(tm, tn, streams, epilogue):
    row_pieces = _row_pieces(tm)
    blocks = [(cols, rows) for cols in _col_subtiles(tn) for rows in row_pieces]

    def stage(cols, rows):
        for u_ref, w_ref, xs_ref, carry_ref in streams:
            if rows.start == 0:
                xs_ref[0:SUBLANES, cols] = carry_ref[:, cols]
            xs_ref[SUBLANES + rows.start:SUBLANES + rows.stop, cols] = _dot(u_ref[rows, :], w_ref[:, cols])
            if rows.stop == tm:
                carry_ref[:, cols] = xs_ref[tm:tm + SUBLANES, cols]

    stage(*blocks[0])
    for idx, (cols, rows) in enumerate(blocks):
        if idx + 1 < len(blocks):
            stage(*blocks[idx + 1])
        epilogue(cols, rows)


def _conv_proj_body(u_ref, w_ref, cs_ref, cw_ref, cb_ref, o_ref, carry_ref, xs_ref):
    @pl.when(pl.program_id(0) == 0)
    def _():
        carry_ref[...] = jnp.zeros_like(carry_ref)

    def epilogue(cols, rows):
        y = _conv_rows(xs_ref, cols, rows, cw_ref[:, cols], cb_ref[:, cols])
        o_ref[rows, cols] = (y * _sigmoid(y) * cs_ref[:, cols]).astype(o_ref.dtype)

    _staged_conv_pipeline(o_ref.shape[0], o_ref.shape[1], [(u_ref, w_ref, xs_ref, carry_ref)], epilogue)


def conv_proj(u, w, w_col_block, colscale, conv_w, conv_b):
    M, K = u.shape
    N = colscale.shape[1]
    tm = _tile(M, ROW_TILE)
    kc = conv_w.shape[0]
    return pl.pallas_call(
        _conv_proj_body,
        grid=(M // tm,),
        in_specs=[pl.BlockSpec((tm, K), lambda i: (i, 0)),
                  pl.BlockSpec((K, N), lambda i: (0, w_col_block)),
                  pl.BlockSpec((1, N), lambda i: (0, 0)),
                  pl.BlockSpec((kc, N), lambda i: (0, 0)),
                  pl.BlockSpec((1, N), lambda i: (0, 0))],
        out_specs=pl.BlockSpec((tm, N), lambda i: (i, 0)),
        out_shape=jax.ShapeDtypeStruct((M, N), BF16),
        scratch_shapes=[pltpu.VMEM((SUBLANES, N), F32), pltpu.VMEM((tm + SUBLANES, N), F32)],
        compiler_params=_cparams(("arbitrary",)),
        name="conv_proj",
    )(u, w, colscale, conv_w, conv_b.reshape(1, N))


def _gate_proj_body(u_ref, w_ref, wt_ref, o_ref, ot_ref):
    u = u_ref[...]
    o_ref[...] = _dot(u, w_ref[...])
    ot_ref[...] = _dot_nt(wt_ref[...], u)


def gate_proj(u, w_pad, wt_pad):
    M, K = u.shape
    tm = _tile(M, 512)
    return pl.pallas_call(
        _gate_proj_body,
        grid=(M // tm,),
        in_specs=[pl.BlockSpec((tm, K), lambda i: (i, 0)),
                  pl.BlockSpec((K, LANES), lambda i: (0, 0)),
                  pl.BlockSpec((SUBLANES, K), lambda i: (0, 0))],
        out_specs=[pl.BlockSpec((tm, LANES), lambda i: (i, 0)),
                   pl.BlockSpec((SUBLANES, tm), lambda i: (0, i))],
        out_shape=[jax.ShapeDtypeStruct((M, LANES), F32), jax.ShapeDtypeStruct((SUBLANES, M), F32)],
        compiler_params=_cparams(("parallel",)),
        name="gate_proj",
    )(u, w_pad, wt_pad)


def _gelu_tanh(x):
    return 0.5 * x * (1.0 + jnp.tanh(math.sqrt(2.0 / math.pi) * (x + 0.044715 * (x * x * x))))


def _ffn_up_body(u_ref, wa_ref, wb_ref, cwa_ref, cba_ref, cwb_ref, cbb_ref, o_ref,
                 carry_a, carry_b, xs_a, xs_b):
    j = pl.program_id(1)

    @pl.when(pl.program_id(0) == 0)
    def _():
        carry_a[j] = jnp.zeros(carry_a.shape[1:], F32)
        carry_b[j] = jnp.zeros(carry_b.shape[1:], F32)

    def epilogue(cols, rows):
        a = _conv_rows(xs_a, cols, rows, cwa_ref[:, cols], cba_ref[:, cols])
        b = _conv_rows(xs_b, cols, rows, cwb_ref[:, cols], cbb_ref[:, cols])
        o_ref[rows, cols] = (_gelu_tanh(a) * b).astype(o_ref.dtype)

    _staged_conv_pipeline(o_ref.shape[0], o_ref.shape[1],
                          [(u_ref, wa_ref, xs_a, carry_a.at[j]), (u_ref, wb_ref, xs_b, carry_b.at[j])], epilogue)


def ffn_up(u, w_up, conv_w, conv_b):
    M, K = u.shape
    F = w_up.shape[1] // 2
    tm = _tile(M, ROW_TILE)
    tn = _tile(F, COL_TILE // 2)
    nj = F // tn
    kc = conv_w.shape[0]
    cb = conv_b.reshape(1, 2 * F)
    return pl.pallas_call(
        _ffn_up_body,
        grid=(M // tm, nj),
        in_specs=[pl.BlockSpec((tm, K), lambda i, j: (i, 0)),
                  pl.BlockSpec((K, tn), lambda i, j: (0, j)),
                  pl.BlockSpec((K, tn), lambda i, j: (0, j + nj)),
                  pl.BlockSpec((kc, tn), lambda i, j: (0, j)),
                  pl.BlockSpec((1, tn), lambda i, j: (0, j)),
                  pl.BlockSpec((kc, tn), lambda i, j: (0, j + nj)),
                  pl.BlockSpec((1, tn), lambda i, j: (0, j + nj))],
        out_specs=pl.BlockSpec((tm, tn), lambda i, j: (i, j)),
        out_shape=jax.ShapeDtypeStruct((M, F), BF16),
        scratch_shapes=[pltpu.VMEM((nj, SUBLANES, tn), F32), pltpu.VMEM((nj, SUBLANES, tn), F32),
                        pltpu.VMEM((tm + SUBLANES, tn), F32), pltpu.VMEM((tm + SUBLANES, tn), F32)],
        compiler_params=_cparams(("arbitrary", "arbitrary")),
        name="ffn_up",
    )(u, w_up, w_up, conv_w, cb, conv_w, cb)


def _diff_attn_body(lam_ref, q_ref, k_ref, v_ref, g_ref, o_ref, acc_ref, l_ref, knorm_ref, *, tq, lam_init):
    qi = pl.program_id(1)
    n_kv = k_ref.shape[0] // tq
    q = q_ref[...]
    qs = (q[:, :HEAD_DIM], q[:, HEAD_DIM:])
    ksl = (slice(0, HEAD_DIM), slice(HEAD_DIM, 2 * HEAD_DIM))

    @pl.when(qi == 0)
    def _():
        def body(c, carry):
            kk = k_ref[pl.ds(pl.multiple_of(c * tq, tq), tq), :].astype(F32)
            sq = kk * kk
            return tuple(jnp.maximum(carry[mp], jnp.max(jnp.sum(sq[:, ksl[mp]], axis=-1, keepdims=True),
                                                        axis=0, keepdims=True)) for mp in range(2))
        zero = jnp.zeros((1, 1), F32)
        norms = lax.fori_loop(0, n_kv, body, (zero, zero))
        for mp in range(2):
            knorm_ref[mp] = jnp.broadcast_to(norms[mp], (1, LANES))

    row = lax.broadcasted_iota(jnp.int32, (tq, tq), 0) // DIFF_CHUNK
    col = lax.broadcasted_iota(jnp.int32, (tq, tq), 1) // DIFF_CHUNK
    visible = col <= row
    start = pl.multiple_of(qi * tq, tq)

    def lane_partial(p):
        out = p[:, 0:LANES]
        for c in range(1, p.shape[1] // LANES):
            out = out + p[:, c * LANES:(c + 1) * LANES]
        return out

    bounds = []
    for mp in range(2):
        qf = qs[mp].astype(F32)
        qn2 = jnp.sum(qf * qf, axis=-1, keepdims=True)
        bounds.append(jnp.sqrt(qn2 * knorm_ref[mp][:, 0:1]))
    kd = k_ref[pl.ds(start, tq), :]
    vd = v_ref[pl.ds(start, tq), :]
    for mp in range(2):
        p = jnp.where(visible, jnp.exp2(_dot_nt(qs[mp], kd[:, ksl[mp]]) - bounds[mp]), 0.0)
        l_ref[mp] = lane_partial(p)
        acc_ref[mp] = _dot(p.astype(BF16), vd)

    def fast_block(kb):
        off = pl.multiple_of(kb * tq, tq)
        kk = k_ref[pl.ds(off, tq), :]
        vv = v_ref[pl.ds(off, tq), :]
        for mp in range(2):
            p = jnp.exp2(_dot_nt(qs[mp], kk[:, ksl[mp]]) - bounds[mp])
            l_ref[mp] += lane_partial(p)
            acc_ref[mp] += _dot(p.astype(BF16), vv)

    def fast_group(n, carry):
        for b in range(DIFF_UNROLL):
            fast_block(DIFF_UNROLL * n + b)
        return carry

    def fast_single(kb, carry):
        fast_block(kb)
        return carry

    n_groups = qi // DIFF_UNROLL
    lax.fori_loop(0, n_groups, fast_group, 0)
    lax.fori_loop(n_groups * DIFF_UNROLL, qi, fast_single, 0)

    l_min = jnp.minimum(jnp.min(jnp.sum(l_ref[0], axis=-1, keepdims=True)),
                        jnp.min(jnp.sum(l_ref[1], axis=-1, keepdims=True)))
    redo = jnp.logical_not(l_min >= DIFF_MIN_NORMALISER)

    @pl.when(redo)
    def _():
        lane0 = lax.broadcasted_iota(jnp.int32, (tq, LANES), 1) == 0
        init = []
        for mp in range(2):
            s = jnp.where(visible, _dot_nt(qs[mp], kd[:, ksl[mp]]), NEG)
            m = jnp.max(s, axis=-1, keepdims=True)
            p = jnp.exp2(s - m)
            acc_ref[mp] = _dot(p.astype(BF16), vd)
            init += [m, jnp.sum(p, axis=-1, keepdims=True)]

        def step(kb, carry):
            off = pl.multiple_of(kb * tq, tq)
            kk = k_ref[pl.ds(off, tq), :]
            vv = v_ref[pl.ds(off, tq), :]
            out = []
            for mp in range(2):
                m, l = carry[2 * mp], carry[2 * mp + 1]
                s = _dot_nt(qs[mp], kk[:, ksl[mp]])
                m_new = jnp.maximum(m, jnp.max(s, axis=-1, keepdims=True))
                alpha = jnp.exp2(m - m_new)
                p = jnp.exp2(s - m_new)
                acc_ref[mp] = alpha * acc_ref[mp] + _dot(p.astype(BF16), vv)
                out += [m_new, alpha * l + jnp.sum(p, axis=-1, keepdims=True)]
            return tuple(out)

        fin = lax.fori_loop(0, qi, step, tuple(init))
        for mp in range(2):
            l_ref[mp] = jnp.where(lane0, fin[2 * mp + 1], 0.0)

    lp = lam_ref[...]
    lam = (jnp.exp(jnp.sum(lp[0:1, :] * lp[1:2, :], axis=-1, keepdims=True))
           - jnp.exp(jnp.sum(lp[2:3, :] * lp[3:4, :], axis=-1, keepdims=True)) + lam_init)
    l0 = jnp.sum(l_ref[0], axis=-1, keepdims=True)
    l1 = jnp.sum(l_ref[1], axis=-1, keepdims=True)
    out = acc_ref[0] / l0 - lam * (acc_ref[1] / l1)
    o_ref[...] = (_rms(out, g_ref[...]) * (1.0 - lam_init)).astype(o_ref.dtype)


def diff_attention(q, k, v, lam_p, g_sub, lam_init, n_heads):
    S = q[0].shape[0]
    W = 2 * HEAD_DIM
    tq = _tile(S, 512)
    return pl.pallas_call(
        functools.partial(_diff_attn_body, tq=tq, lam_init=lam_init),
        grid=(n_heads, S // tq),
        in_specs=[pl.BlockSpec((4, HEAD_DIM), lambda h, i: (0, 0)),
                  pl.BlockSpec((tq, W), lambda h, i: (i, q[1] // W + h)),
                  pl.BlockSpec((S, W), lambda h, i: (0, k[1] // W + h)),
                  pl.BlockSpec((S, W), lambda h, i: (0, v[1] // W + h)),
                  pl.BlockSpec((1, W), lambda h, i: (0, 0))],
        out_specs=pl.BlockSpec((tq, W), lambda h, i: (i, h)),
        out_shape=jax.ShapeDtypeStruct((S, n_heads * W), BF16),
        scratch_shapes=[pltpu.VMEM((2, tq, W), F32), pltpu.VMEM((2, tq, LANES), F32),
                        pltpu.VMEM((2, 1, LANES), F32)],
        compiler_params=_cparams(("parallel", "arbitrary")),
        name="diff_attention",
    )(lam_p, q[0], k[0], v[0], g_sub.reshape(1, W))


def _stick_breaking_body(q_ref, k_ref, v_ref, o_ref, acc_ref, *, tq, heads):
    qi = pl.program_id(1)
    row = lax.broadcasted_iota(jnp.int32, (tq, tq), 0)
    col = lax.broadcasted_iota(jnp.int32, (tq, tq), 1)
    strict = col < row
    suffix_ones = (row >= col).astype(BF16)

    def scores(hh, kb, diag):
        sl = slice(hh * HEAD_DIM, (hh + 1) * HEAD_DIM)
        off = pl.multiple_of(kb * tq, tq)
        z = _dot_nt(q_ref[:, sl], k_ref[pl.ds(off, tq), sl])
        log_keep = -(jnp.maximum(z, 0.0) + jnp.log1p(jnp.exp(-jnp.abs(z))))
        if diag:
            log_keep = jnp.where(strict, log_keep, 0.0)
        hi = log_keep.astype(BF16)
        lo = (log_keep - hi.astype(F32)).astype(BF16)
        cum = _dot(hi, suffix_ones) + _dot(lo, suffix_ones)
        return z + cum, cum[:, 0:1], v_ref[pl.ds(off, tq), sl]

    has_prev = qi > 0
    prev = jnp.maximum(qi - 1, 0)
    runs = []
    for hh in range(heads):
        zc_d, tot_d, v_d = scores(hh, qi, True)
        zc_p, tot_p, v_p = scores(hh, prev, False)
        a_d = jnp.where(strict, jnp.exp(zc_d), 0.0)
        a_p = jnp.where(has_prev, jnp.exp(zc_p + tot_d), 0.0)
        acc_ref[hh] = _dot(a_d.astype(BF16), v_d) + _dot(a_p.astype(BF16), v_p)
        runs.append(tot_d + jnp.where(has_prev, tot_p, 0.0))

    def needs_more(rs):
        return functools.reduce(jnp.maximum, [jnp.max(r) for r in rs]) > SB_LOG_FLOOR

    def cond(state):
        return jnp.logical_and(state[0] >= 0, state[1])

    def step(state):
        kb = state[0]
        new_runs = []
        for hh in range(heads):
            run = state[2 + hh]
            zc, tot, vv = scores(hh, kb, False)
            acc_ref[hh] += _dot(jnp.exp(zc + run).astype(BF16), vv)
            new_runs.append(run + tot)
        return (kb - 1, needs_more(new_runs), *new_runs)

    lax.while_loop(cond, step, (qi - 2, needs_more(runs), *runs))
    for hh in range(heads):
        o_ref[:, hh * HEAD_DIM:(hh + 1) * HEAD_DIM] = acc_ref[hh].astype(o_ref.dtype)


def stick_breaking(q, k, v, n_heads):
    S = q[0].shape[0]
    heads = 2
    W = heads * HEAD_DIM
    tq = _tile(S, 256)
    return pl.pallas_call(
        functools.partial(_stick_breaking_body, tq=tq, heads=heads),
        grid=(n_heads // heads, S // tq),
        in_specs=[pl.BlockSpec((tq, W), lambda h, i: (i, q[1] // W + h)),
                  pl.BlockSpec((S, W), lambda h, i: (0, k[1] // W + h)),
                  pl.BlockSpec((S, W), lambda h, i: (0, v[1] // W + h))],
        out_specs=pl.BlockSpec((tq, W), lambda h, i: (i, h)),
        out_shape=jax.ShapeDtypeStruct((S, n_heads * HEAD_DIM), BF16),
        scratch_shapes=[pltpu.VMEM((heads, tq, HEAD_DIM), F32)],
        compiler_params=_cparams(("parallel", "parallel")),
        name="stick_breaking",
    )(q[0], k[0], v[0])


def _hi_lo(x):
    hi = x.astype(BF16)
    return hi, (x - hi.astype(F32)).astype(BF16)


def _mlstm_body(q_ref, k_ref, v_ref, og_ref, gc_ref, gr_ref, bc_ref, br_ref, gn_ref, o_ref,
                c_ref, n_ref, m_ref, *, n_heads, L):
    dv = 2 * HEAD_DIM

    @pl.when(pl.program_id(0) == 0)
    def _():
        c_ref[...] = jnp.zeros_like(c_ref)
        n_ref[...] = jnp.zeros_like(n_ref)
        m_ref[...] = jnp.zeros_like(m_ref)

    row = lax.broadcasted_iota(jnp.int32, (L, L), 0)
    col = lax.broadcasted_iota(jnp.int32, (L, L), 1)
    tril = col <= row
    tril_ones = tril.astype(BF16)
    triu_ones = (row <= col).astype(BF16)

    gc = gc_ref[...] + bc_ref[...]
    gr = gr_ref[...] + br_ref[...]
    hi, lo = _hi_lo(_log_sigmoid(gc))
    cum_c = _dot(tril_ones, hi) + _dot(tril_ones, lo)
    hi, lo = _hi_lo(_log_sigmoid(gr))
    cum_r = _dot(hi, triu_ones) + _dot(lo, triu_ones)

    gnorm = gn_ref[...]
    for h in range(n_heads):
        b_c = cum_c[:, n_heads + h:n_heads + h + 1]
        li_c = gc[:, h:h + 1]
        b_r = cum_r[n_heads + h:n_heads + h + 1, :]
        li_r = gr[h:h + 1, :]
        m_prev = m_ref[:, h:h + 1]

        dm = jnp.where(tril, b_c - b_r + li_r, NEG)
        inter = b_c + m_prev
        m_t = jnp.maximum(inter, jnp.max(dm, axis=-1, keepdims=True))
        w_intra = jnp.exp(dm - m_t)
        w_inter = jnp.exp(inter - m_t)

        qh = q_ref[:, h * HEAD_DIM:(h + 1) * HEAD_DIM]
        kh = k_ref[:, h * HEAD_DIM:(h + 1) * HEAD_DIM]
        vh = v_ref[:, h * dv:(h + 1) * dv]
        s = _dot_nt(qh, kh) * w_intra
        num = _dot(s.astype(BF16), vh) + w_inter * _dot(qh, c_ref[h].astype(BF16))
        nq = (jnp.sum(s, axis=-1, keepdims=True)
              + w_inter * jnp.sum(qh.astype(F32) * n_ref[h], axis=-1, keepdims=True))
        den = jnp.maximum(jnp.abs(nq), jnp.exp(-m_t))
        hh = num / den

        bl = b_c[L - 1:L, :]
        g_c = bl - b_c + li_c
        m_new = jnp.maximum(bl + m_prev, jnp.max(g_c, axis=0, keepdims=True))
        wk = jnp.exp(g_c - m_new)
        dec = jnp.exp(bl + m_prev - m_new)
        kw = kh.astype(F32) * wk
        c_ref[h] = dec * c_ref[h] + _dot(kw.T.astype(BF16), vh)
        n_ref[h] = dec * n_ref[h] + jnp.sum(kw, axis=0, keepdims=True)
        m_ref[:, h:h + 1] = m_new

        og = og_ref[:, h * dv:(h + 1) * dv].astype(F32)
        o_ref[:, h * dv:(h + 1) * dv] = (_rms(hh, gnorm) * _sigmoid(og)).astype(o_ref.dtype)


def mlstm(q, k, v, og, gates_c, gates_r, bias_c, bias_r, g_norm, n_heads):
    S = q[0].shape[0]
    L = _tile(S, REC_CHUNK)
    qk_w = n_heads * HEAD_DIM
    v_w = n_heads * 2 * HEAD_DIM
    return pl.pallas_call(
        functools.partial(_mlstm_body, n_heads=n_heads, L=L),
        grid=(S // L,),
        in_specs=[pl.BlockSpec((L, qk_w), lambda c: (c, q[1] // qk_w)),
                  pl.BlockSpec((L, qk_w), lambda c: (c, k[1] // qk_w)),
                  pl.BlockSpec((L, v_w), lambda c: (c, v[1] // v_w)),
                  pl.BlockSpec((L, v_w), lambda c: (c, og[1] // v_w)),
                  pl.BlockSpec((L, LANES), lambda c: (c, 0)),
                  pl.BlockSpec((SUBLANES, L), lambda c: (0, c)),
                  pl.BlockSpec((1, LANES), lambda c: (0, 0)),
                  pl.BlockSpec((SUBLANES, 1), lambda c: (0, 0)),
                  pl.BlockSpec((1, 2 * HEAD_DIM), lambda c: (0, 0))],
        out_specs=pl.BlockSpec((L, v_w), lambda c: (c, 0)),
        out_shape=jax.ShapeDtypeStruct((S, v_w), BF16),
        scratch_shapes=[pltpu.VMEM((n_heads, HEAD_DIM, 2 * HEAD_DIM), F32),
                        pltpu.VMEM((n_heads, 1, HEAD_DIM), F32),
                        pltpu.VMEM((1, LANES), F32)],
        compiler_params=_cparams(("arbitrary",)),
        name="mlstm",
    )(q[0], k[0], v[0], og[0], gates_c, gates_r, bias_c, bias_r, g_norm.reshape(1, 2 * HEAD_DIM))


def _retention_body(q_ref, k_ref, v_ref, cg_ref, gn_ref, o_ref, r_ref, dmask_ref, *, n_heads, L):
    dv = 2 * HEAD_DIM
    log_g = [math.log(1.0 - 2.0 ** (-5.0 - h)) for h in range(n_heads)]

    @pl.when(pl.program_id(0) == 0)
    def _():
        r_ref[...] = jnp.zeros_like(r_ref)
        diff = (lax.broadcasted_iota(jnp.int32, (L, L), 0)
                - lax.broadcasted_iota(jnp.int32, (L, L), 1)).astype(F32)
        for h in range(n_heads):
            dmask_ref[h] = jnp.where(diff >= 0, jnp.exp(jnp.maximum(diff, 0.0) * log_g[h]), 0.0)

    idx = lax.broadcasted_iota(jnp.int32, (L, 1), 0).astype(F32)
    gnorm = gn_ref[...]
    for h in range(n_heads):
        q_dec = jnp.exp((idx + 1.0) * log_g[h])
        k_dec = jnp.exp((L - 1.0 - idx) * log_g[h])
        c_dec = math.exp(L * log_g[h])
        qh = q_ref[:, h * HEAD_DIM:(h + 1) * HEAD_DIM]
        kh = k_ref[:, h * HEAD_DIM:(h + 1) * HEAD_DIM]
        vh = v_ref[:, h * dv:(h + 1) * dv]
        s = _dot_nt(qh, kh) * dmask_ref[h]
        o = _dot(s.astype(BF16), vh) + q_dec * _dot(qh, r_ref[h].astype(BF16))
        kd = kh.astype(F32) * k_dec
        r_ref[h] = c_dec * r_ref[h] + _dot(kd.T.astype(BF16), vh)
        cg = cg_ref[:, h * dv:(h + 1) * dv].astype(F32)
        o_ref[:, h * dv:(h + 1) * dv] = (_rms(o, gnorm) * (cg * _sigmoid(cg))).astype(o_ref.dtype)


def retention(q, k, v, cg, g_norm, n_heads):
    S = q[0].shape[0]
    L = _tile(S, REC_CHUNK)
    qk_w = n_heads * HEAD_DIM
    v_w = n_heads * 2 * HEAD_DIM
    return pl.pallas_call(
        functools.partial(_retention_body, n_heads=n_heads, L=L),
        grid=(S // L,),
        in_specs=[pl.BlockSpec((L, qk_w), lambda c: (c, q[1] // qk_w)),
                  pl.BlockSpec((L, qk_w), lambda c: (c, k[1] // qk_w)),
                  pl.BlockSpec((L, v_w), lambda c: (c, v[1] // v_w)),
                  pl.BlockSpec((L, v_w), lambda c: (c, cg[1] // v_w)),
                  pl.BlockSpec((1, 2 * HEAD_DIM), lambda c: (0, 0))],
        out_specs=pl.BlockSpec((L, v_w), lambda c: (c, 0)),
        out_shape=jax.ShapeDtypeStruct((S, v_w), BF16),
        scratch_shapes=[pltpu.VMEM((n_heads, HEAD_DIM, 2 * HEAD_DIM), F32),
                        pltpu.VMEM((n_heads, L, L), F32)],
        compiler_params=_cparams(("arbitrary",)),
        name="retention",
    )(q[0], k[0], v[0], cg[0], g_norm.reshape(1, 2 * HEAD_DIM))


def _rope_tables(S):
    inv = ROPE_THETA ** (-jnp.arange(0, HEAD_DIM, 2, dtype=F32) / HEAD_DIM)
    ang = jnp.arange(S, dtype=F32)[:, None] * inv[None, :]
    cos, sin = jnp.cos(ang), jnp.sin(ang)
    return jnp.concatenate([cos, cos], axis=-1), jnp.concatenate([-sin, sin], axis=-1)


def _col_scale(n, scaled_ranges):
    cs = jnp.ones((1, n), F32)
    for lo, hi, val in scaled_ranges:
        cs = cs.at[:, lo:hi].set(val)
    return cs


def kernel(x, p, w_in_even, w_out_even, diff_lambda, g_diff_sub, w_conv_qk, b_conv_qk, b_igate, b_fgate, g_mlstm_head, w_in_odd, w_out_odd, g_ret_head, g_mix_pre, g_mix_post, g_ffn_pre, g_ffn_post, w_ffn_up, w_ffn_conv, b_ffn_conv, w_ffn_down, g_ple, w_ple_gate_down, w_ple_gate_up, w_ple_proj):
    B, S, D = x.shape
    assert B == 1
    depth = g_mix_pre.shape[0]
    n4 = D // 1024
    n_sb = D // 512
    qk4 = n4 * HEAD_DIM
    v4 = n4 * 2 * HEAD_DIM
    assert 2 * qk4 == v4 == n_sb * HEAD_DIM
    inv_sqrt_d = HEAD_DIM ** -0.5

    cosf, sins = _rope_tables(S)
    cs_even_rope = _col_scale(2 * v4, [(0, v4, inv_sqrt_d * LOG2E)])
    cs_even_conv = _col_scale(v4, [(0, qk4, inv_sqrt_d)])
    cs_odd_rope = _col_scale(v4, [(qk4, 2 * qk4, inv_sqrt_d)])
    cs_odd_plain = _col_scale(5 * v4, [(2 * v4, 3 * v4, inv_sqrt_d)])

    tn = _tile(v4, COL_TILE)
    tiles_v4 = v4 // tn
    p_all = p.reshape(depth * S, p.shape[-1])

    h = x.reshape(S, D)
    u = first_norm(h, g_mix_pre[0])
    for i in range(depth):
        j = i // 2
        if i % 2 == 0:
            lam_init = 0.8 - 0.6 * math.exp(-0.3 * i)
            w_in = w_in_even[j]
            w_main = w_in[:, :6 * v4].astype(BF16)
            qk_a = rope_proj(u, w_main, cs_even_rope, cosf, sins)
            plain = matmul([(u, w_main, 0)], BF16, n_out=3 * v4, tn=tn,
                           w_col_tile=lambda c: c + 2 * tiles_v4 + jnp.where(c >= tiles_v4, tiles_v4, 0),
                           name="in_proj_plain")
            qk_b = conv_proj(u, w_main, 3, cs_even_conv, w_conv_qk[j], b_conv_qk[j])
            w_gate = w_in[:, 6 * v4:]
            w_gate_c = jnp.pad(w_gate, ((0, 0), (0, LANES - 2 * n4))).astype(BF16)
            w_gate_r = jnp.pad(w_gate.T, ((0, SUBLANES - 2 * n4), (0, 0))).astype(BF16)
            gates_c, gates_r = gate_proj(u, w_gate_c, w_gate_r)
            bias = jnp.concatenate([b_igate[j], b_fgate[j]])
            bias_c = jnp.pad(bias, (0, LANES - 2 * n4)).reshape(1, LANES)
            bias_r = jnp.pad(bias, (0, SUBLANES - 2 * n4)).reshape(SUBLANES, 1)
            ya = diff_attention((qk_a, 0), (qk_a, v4), (plain, 0), diff_lambda[j], g_diff_sub[j], lam_init, n4)
            yb = mlstm((qk_b, 0), (qk_b, qk4), (plain, v4), (plain, 2 * v4),
                       gates_c, gates_r, bias_c, bias_r, g_mlstm_head[j], n4)
            w_out = w_out_even[j].astype(BF16)
        else:
            w_main = w_in_odd[j].astype(BF16)
            qk_c = rope_proj(u, w_main, cs_odd_rope, cosf, sins)
            plain = matmul([(u, w_main, 0)], BF16, colscale=cs_odd_plain, n_out=5 * v4, tn=tn,
                           w_col_tile=lambda c: c + tiles_v4, name="in_proj_plain")
            ya = retention((qk_c, 0), (qk_c, qk4), (plain, 0), (plain, v4), g_ret_head[j], n4)
            yb = stick_breaking((plain, 2 * v4), (plain, 3 * v4), (plain, 4 * v4), n_sb)
            w_out = w_out_odd[j].astype(BF16)
        mix = matmul([(ya, w_out, 0), (yb, w_out, 1)], F32, name="out_proj")
        h, u = residual_norm(h, mix, g_mix_post[i], g_ffn_pre[i])
        hid = ffn_up(u, w_ffn_up[i].astype(BF16), w_ffn_conv[i], b_ffn_conv[i])
        f = matmul([(hid, w_ffn_down[i].astype(BF16), 0)], F32, name="ffn_down")
        g_next = g_mix_pre[i + 1] if i + 1 < depth else None
        h, u = ffn_residual_ple(h, f, p_all, i, g_ffn_post[i], g_ple[i], w_ple_gate_down[i].astype(BF16),
                                w_ple_gate_up[i].astype(BF16), w_ple_proj[i].astype(BF16), g_next)
    return h.reshape(B, S, D)
```

```python
import functools
import math

import jax
import jax.numpy as jnp
from jax import lax
from jax.experimental import pallas as pl
from jax.experimental.pallas import tpu as pltpu

F32 = jnp.float32
BF16 = jnp.bfloat16

HEAD_DIM = 128
ROPE_THETA = 10000.0
EPS = 1e-6
DIFF_CHUNK = 64
DIFF_UNROLL = 4
REC_CHUNK = 256
NEG = -1e30
LOG2E = math.log2(math.e)
DIFF_MIN_NORMALISER = 2.0 ** -80
SB_LOG_FLOOR = -120.0
VMEM_LIMIT = 56 * 1024 * 1024
LANES = 128
SUBLANES = 8
MXU_COLS = 256
ROW_TILE = 1024
COL_TILE = 1024
NORM_ROWS = 64
ROW_PIECE = 512


def _cparams(sem):
    return pltpu.CompilerParams(dimension_semantics=sem, vmem_limit_bytes=VMEM_LIMIT)


def _dot(a, b):
    return jnp.dot(a, b, preferred_element_type=F32)


def _dot_nt(a, b):
    return lax.dot_general(a, b, (((1,), (1,)), ((), ())), preferred_element_type=F32)


def _rms(x, g):
    return x * lax.rsqrt(jnp.mean(x * x, axis=-1, keepdims=True) + EPS) * g


def _log_sigmoid(x):
    return jnp.minimum(x, 0.0) - jnp.log1p(jnp.exp(-jnp.abs(x)))


def _sigmoid(x):
    return 1.0 / (1.0 + jnp.exp(-x))


def _tile(n, pref):
    if n <= pref:
        return n
    t = pref - pref % MXU_COLS
    while n % t:
        t -= MXU_COLS
    assert t > 0, (n, pref)
    return t


def _layer_spec(block, index_fn, layer):
    return pl.BlockSpec((None,) + block, lambda *g: (layer,) + index_fn(*g))


def _col_subtiles(tn):
    w = min(tn, MXU_COLS)
    assert tn % w == 0
    return [slice(c * w, (c + 1) * w) for c in range(tn // w)]


def _norm_body(x_ref, g_ref, u_ref):
    u_ref[...] = _rms(x_ref[...], g_ref[...]).astype(BF16)


def first_norm(x, g):
    M, D = x.shape
    tm = _tile(M, 256)
    return pl.pallas_call(
        _norm_body,
        grid=(M // tm,),
        in_specs=[pl.BlockSpec((tm, D), lambda i: (i, 0)),
                  pl.BlockSpec((1, D), lambda i: (0, 0))],
        out_specs=pl.BlockSpec((tm, D), lambda i: (i, 0)),
        out_shape=jax.ShapeDtypeStruct((M, D), BF16),
        compiler_params=_cparams(("parallel",)),
        name="first_norm",
    )(x, g.reshape(1, D))


def _ffn_residual_ple_body(h_ref, f_ref, p_ref, gp_ref, gple_ref, wd_ref, wu_ref, wp_ref, gn_ref,
                           ho_ref, *maybe_u_ref):
    h2 = h_ref[...] + _rms(f_ref[...], gp_ref[...])
    t = _dot(_rms(h2, gple_ref[...]).astype(BF16), wd_ref[...])
    gate = _sigmoid(_dot(t.astype(BF16), wu_ref[...]))
    h3 = h2 + gate * _dot(p_ref[...].astype(BF16), wp_ref[...])
    ho_ref[...] = h3
    if maybe_u_ref:
        maybe_u_ref[0][...] = _rms(h3, gn_ref[...]).astype(BF16)


def ffn_residual_ple(h, f, p_all, layer, g_post, g_ple, w_down, w_up, w_proj, g_next):
    M, D = h.shape
    R = w_down.shape[2]
    P = p_all.shape[1]
    tm = _tile(M, 256)
    p_block0 = layer * (M // tm)
    row = pl.BlockSpec((tm, D), lambda i: (i, 0))
    vec = pl.BlockSpec((1, D), lambda i: (0, 0))
    want_u = g_next is not None
    out_specs = [row, row] if want_u else [row]
    out_shape = [jax.ShapeDtypeStruct((M, D), F32)]
    if want_u:
        out_shape.append(jax.ShapeDtypeStruct((M, D), BF16))
    gn = g_next if want_u else g_ple
    res = pl.pallas_call(
        _ffn_residual_ple_body,
        grid=(M // tm,),
        in_specs=[row, row,
                  pl.BlockSpec((tm, P), lambda i: (p_block0 + i, 0)),
                  vec, vec,
                  _layer_spec((D, R), lambda i: (0, 0), layer),
                  _layer_spec((R, D), lambda i: (0, 0), layer),
                  _layer_spec((P, D), lambda i: (0, 0), layer),
                  vec],
        out_specs=out_specs,
        out_shape=out_shape,
        compiler_params=_cparams(("parallel",)),
        name="ffn_residual_ple",
    )(h, f, p_all, g_post.reshape(1, D), g_ple.reshape(1, D), w_down, w_up, w_proj, gn.reshape(1, D))
    return (res[0], res[1]) if want_u else (res[0], None)


def _matmul_body(*refs, n_pairs, scaled):
    o_ref = refs[-1]
    for cols in _col_subtiles(o_ref.shape[1]):
        acc = _dot(refs[0][...], refs[1][:, cols])
        for q in range(1, n_pairs):
            acc = acc + _dot(refs[2 * q][...], refs[2 * q + 1][:, cols])
        if scaled:
            acc = acc * refs[2 * n_pairs][:, cols]
        o_ref[:, cols] = acc.astype(o_ref.dtype)


def matmul(pairs, out_dtype, colscale=None, n_out=None, tn=None, w_col_tile=lambda j: j, name="matmul"):
    M = pairs[0][0].shape[0]
    N = pairs[0][1][0].shape[2] if n_out is None else n_out
    tm = _tile(M, ROW_TILE)
    tn = _tile(N, COL_TILE) if tn is None else tn
    in_specs, args = [], []
    for x, (w_stack, layer), r in pairs:
        K = x.shape[1]
        in_specs += [pl.BlockSpec((tm, K), lambda i, j: (i, 0)),
                     _layer_spec((K, tn), lambda i, j, r=r: (r, w_col_tile(j)), layer)]
        args += [x, w_stack]
    if colscale is not None:
        in_specs.append(pl.BlockSpec((1, tn), lambda i, j: (0, j)))
        args.append(colscale)
    return pl.pallas_call(
        functools.partial(_matmul_body, n_pairs=len(pairs), scaled=colscale is not None),
        grid=(M // tm, N // tn),
        in_specs=in_specs,
        out_specs=pl.BlockSpec((tm, tn), lambda i, j: (i, j)),
        out_shape=jax.ShapeDtypeStruct((M, N), out_dtype),
        compiler_params=_cparams(("parallel", "arbitrary")),
        name=name,
    )(*args)


def _out_proj_residual_body(xa_ref, xb_ref, wa_ref, wb_ref, h_ref, gp_ref, gn_ref, ho_ref, u_ref, y_ref):
    tm = ho_ref.shape[0]
    for cols in _col_subtiles(ho_ref.shape[1]):
        y_ref[:, cols] = _dot(xa_ref[...], wa_ref[:, cols]) + _dot(xb_ref[...], wb_ref[:, cols])
    rc = min(tm, NORM_ROWS)
    for r0 in range(0, tm, rc):
        r = slice(r0, r0 + rc)
        hn = h_ref[r, :] + _rms(y_ref[r, :], gp_ref[...])
        ho_ref[r, :] = hn
        u_ref[r, :] = _rms(hn, gn_ref[...]).astype(u_ref.dtype)


def out_proj_residual(xa, xb, w, h, g_post, g_next):
    M, D = h.shape
    K = xa.shape[1]
    tm = _tile(M, 256)
    row = pl.BlockSpec((tm, D), lambda i: (i, 0))
    vec = pl.BlockSpec((1, D), lambda i: (0, 0))
    xspec = pl.BlockSpec((tm, K), lambda i: (i, 0))

    def wspec(row_block):
        return pl.BlockSpec((None, K, D), lambda i: (w[1], row_block, 0), pipeline_mode=pl.Buffered(1))

    return pl.pallas_call(
        _out_proj_residual_body,
        grid=(M // tm,),
        in_specs=[xspec, xspec, wspec(0), wspec(1), row, vec, vec],
        out_specs=[row, row],
        out_shape=[jax.ShapeDtypeStruct((M, D), F32), jax.ShapeDtypeStruct((M, D), BF16)],
        scratch_shapes=[pltpu.VMEM((tm, D), F32)],
        compiler_params=_cparams(("parallel",)),
        name="out_proj_residual",
    )(xa, xb, w[0], w[0], h, g_post.reshape(1, D), g_next.reshape(1, D))


def _rope_proj_body(u_ref, w_ref, cs_ref, cos_ref, sin_ref, o_ref):
    cos = cos_ref[...]
    sin = sin_ref[...]
    u = u_ref[...]
    for cols in _col_subtiles(o_ref.shape[1]):
        a = _dot(u, w_ref[:, cols]) * cs_ref[:, cols]
        for hh in range((cols.stop - cols.start) // HEAD_DIM):
            x = a[:, hh * HEAD_DIM:(hh + 1) * HEAD_DIM]
            lo = cols.start + hh * HEAD_DIM
            o_ref[:, lo:lo + HEAD_DIM] = (x * cos + pltpu.roll(x, HEAD_DIM // 2, 1) * sin).astype(o_ref.dtype)


def rope_proj(u, w, colscale, cosf, sins):
    M, K = u.shape
    N = colscale.shape[1]
    tm = _tile(M, ROW_TILE)
    tn = _tile(N, COL_TILE)
    return pl.pallas_call(
        _rope_proj_body,
        grid=(M // tm, N // tn),
        in_specs=[pl.BlockSpec((tm, K), lambda i, j: (i, 0)),
                  _layer_spec((K, tn), lambda i, j: (0, j), w[1]),
                  pl.BlockSpec((1, tn), lambda i, j: (0, j)),
                  pl.BlockSpec((tm, HEAD_DIM), lambda i, j: (i, 0)),
                  pl.BlockSpec((tm, HEAD_DIM), lambda i, j: (i, 0))],
        out_specs=pl.BlockSpec((tm, tn), lambda i, j: (i, j)),
        out_shape=jax.ShapeDtypeStruct((M, N), BF16),
        compiler_params=_cparams(("parallel", "arbitrary")),
        name="rope_proj",
    )(u, w[0], colscale, cosf, sins)


def _row_pieces(tm):
    rp = min(tm, ROW_PIECE)
    assert tm % rp == 0
    return [slice(r, r + rp) for r in range(0, tm, rp)]


def _conv_rows(xs_ref, cols, rows, w, b):
    K = w.shape[0]
    lo, hi = SUBLANES + rows.start, SUBLANES + rows.stop
    y = b + w[K - 1:K, :] * xs_ref[lo:hi, cols]
    for d in range(1, K):
        y = y + w[K - 1 - d:K - d, :] * xs_ref[lo - d:hi - d, cols]
    return y


def _staged_conv_pipeline(tm, tn, streams, epilogue):
    row_pieces = _row_pieces(tm)
    blocks = [(cols, rows) for cols in _col_subtiles(tn) for rows in row_pieces]

    def stage(cols, rows):
        for u_ref, w_ref, xs_ref, carry_ref in streams:
            if rows.start == 0:
                xs_ref[0:SUBLANES, cols] = carry_ref[:, cols]
            xs_ref[SUBLANES + rows.start:SUBLANES + rows.stop, cols] = _dot(u_ref[rows, :], w_ref[:, cols])
            if rows.stop == tm:
                carry_ref[:, cols] = xs_ref[tm:tm + SUBLANES, cols]

    stage(*blocks[0])
    for idx, (cols, rows) in enumerate(blocks):
        if idx + 1 < len(blocks):
            stage(*blocks[idx + 1])
        epilogue(cols, rows)


def _conv_proj_body(u_ref, w_ref, cs_ref, cw_ref, cb_ref, o_ref, carry_ref, xs_ref):
    @pl.when(pl.program_id(0) == 0)
    def _():
        carry_ref[...] = jnp.zeros_like(carry_ref)

    def epilogue(cols, rows):
        y = _conv_rows(xs_ref, cols, rows, cw_ref[:, cols], cb_ref[:, cols])
        o_ref[rows, cols] = (y * _sigmoid(y) * cs_ref[:, cols]).astype(o_ref.dtype)

    _staged_conv_pipeline(o_ref.shape[0], o_ref.shape[1], [(u_ref, w_ref, xs_ref, carry_ref)], epilogue)


def conv_proj(u, w, w_col_block, colscale, conv_w, conv_b):
    M, K = u.shape
    N = colscale.shape[1]
    tm = _tile(M, ROW_TILE)
    kc = conv_w.shape[0]
    return pl.pallas_call(
        _conv_proj_body,
        grid=(M // tm,),
        in_specs=[pl.BlockSpec((tm, K), lambda i: (i, 0)),
                  _layer_spec((K, N), lambda i: (0, w_col_block), w[1]),
                  pl.BlockSpec((1, N), lambda i: (0, 0)),
                  pl.BlockSpec((kc, N), lambda i: (0, 0)),
                  pl.BlockSpec((1, N), lambda i: (0, 0))],
        out_specs=pl.BlockSpec((tm, N), lambda i: (i, 0)),
        out_shape=jax.ShapeDtypeStruct((M, N), BF16),
        scratch_shapes=[pltpu.VMEM((SUBLANES, N), F32), pltpu.VMEM((tm + SUBLANES, N), F32)],
        compiler_params=_cparams(("arbitrary",)),
        name="conv_proj",
    )(u, w[0], colscale, conv_w, conv_b.reshape(1, N))


def _gate_proj_body(u_ref, w_ref, wt_ref, o_ref, ot_ref):
    u = u_ref[...]
    o_ref[...] = _dot(u, w_ref[...])
    ot_ref[...] = _dot_nt(wt_ref[...], u)


def gate_proj(u, w_pad, wt_pad):
    M, K = u.shape
    tm = _tile(M, 512)
    return pl.pallas_call(
        _gate_proj_body,
        grid=(M // tm,),
        in_specs=[pl.BlockSpec((tm, K), lambda i: (i, 0)),
                  pl.BlockSpec((K, LANES), lambda i: (0, 0)),
                  pl.BlockSpec((SUBLANES, K), lambda i: (0, 0))],
        out_specs=[pl.BlockSpec((tm, LANES), lambda i: (i, 0)),
                   pl.BlockSpec((SUBLANES, tm), lambda i: (0, i))],
        out_shape=[jax.ShapeDtypeStruct((M, LANES), F32), jax.ShapeDtypeStruct((SUBLANES, M), F32)],
        compiler_params=_cparams(("parallel",)),
        name="gate_proj",
    )(u, w_pad, wt_pad)


def _gelu_tanh(x):
    return 0.5 * x * (1.0 + jnp.tanh(math.sqrt(2.0 / math.pi) * (x + 0.044715 * (x * x * x))))


def _ffn_up_body(u_ref, wa_ref, wb_ref, cwa_ref, cba_ref, cwb_ref, cbb_ref, o_ref,
                 carry_a, carry_b, xs_a, xs_b):
    j = pl.program_id(1)

    @pl.when(pl.program_id(0) == 0)
    def _():
        carry_a[j] = jnp.zeros(carry_a.shape[1:], F32)
        carry_b[j] = jnp.zeros(carry_b.shape[1:], F32)

    def epilogue(cols, rows):
        a = _conv_rows(xs_a, cols, rows, cwa_ref[:, cols], cba_ref[:, cols])
        b = _conv_rows(xs_b, cols, rows, cwb_ref[:, cols], cbb_ref[:, cols])
        o_ref[rows, cols] = (_gelu_tanh(a) * b).astype(o_ref.dtype)

    _staged_conv_pipeline(o_ref.shape[0], o_ref.shape[1],
                          [(u_ref, wa_ref, xs_a, carry_a.at[j]), (u_ref, wb_ref, xs_b, carry_b.at[j])], epilogue)


def ffn_up(u, w_up, conv_w, conv_b):
    M, K = u.shape
    F = w_up[0].shape[2] // 2
    tm = _tile(M, ROW_TILE)
    tn = _tile(F, COL_TILE // 2)
    nj = F // tn
    kc = conv_w.shape[0]
    cb = conv_b.reshape(1, 2 * F)
    return pl.pallas_call(
        _ffn_up_body,
        grid=(M // tm, nj),
        in_specs=[pl.BlockSpec((tm, K), lambda i, j: (i, 0)),
                  _layer_spec((K, tn), lambda i, j: (0, j), w_up[1]),
                  _layer_spec((K, tn), lambda i, j: (0, j + nj), w_up[1]),
                  pl.BlockSpec((kc, tn), lambda i, j: (0, j)),
                  pl.BlockSpec((1, tn), lambda i, j: (0, j)),
                  pl.BlockSpec((kc, tn), lambda i, j: (0, j + nj)),
                  pl.BlockSpec((1, tn), lambda i, j: (0, j + nj))],
        out_specs=pl.BlockSpec((tm, tn), lambda i, j: (i, j)),
        out_shape=jax.ShapeDtypeStruct((M, F), BF16),
        scratch_shapes=[pltpu.VMEM((nj, SUBLANES, tn), F32), pltpu.VMEM((nj, SUBLANES, tn), F32),
                        pltpu.VMEM((tm + SUBLANES, tn), F32), pltpu.VMEM((tm + SUBLANES, tn), F32)],
        compiler_params=_cparams(("arbitrary", "arbitrary")),
        name="ffn_up",
    )(u, w_up[0], w_up[0], conv_w, cb, conv_w, cb)


def _diff_attn_body(lam_ref, q_ref, k_ref, v_ref, g_ref, o_ref, acc_ref, l_ref, knorm_ref, *, tq, lam_init):
    qi = pl.program_id(1)
    n_kv = k_ref.shape[0] // tq
    q = q_ref[...]
    qs = (q[:, :HEAD_DIM], q[:, HEAD_DIM:])
    ksl = (slice(0, HEAD_DIM), slice(HEAD_DIM, 2 * HEAD_DIM))

    @pl.when(qi == 0)
    def _():
        def body(c, carry):
            kk = k_ref[pl.ds(pl.multiple_of(c * tq, tq), tq), :].astype(F32)
            sq = kk * kk
            return tuple(jnp.maximum(carry[mp], jnp.max(jnp.sum(sq[:, ksl[mp]], axis=-1, keepdims=True),
                                                        axis=0, keepdims=True)) for mp in range(2))
        zero = jnp.zeros((1, 1), F32)
        norms = lax.fori_loop(0, n_kv, body, (zero, zero))
        for mp in range(2):
            knorm_ref[mp] = jnp.broadcast_to(norms[mp], (1, LANES))

    row = lax.broadcasted_iota(jnp.int32, (tq, tq), 0) // DIFF_CHUNK
    col = lax.broadcasted_iota(jnp.int32, (tq, tq), 1) // DIFF_CHUNK
    visible = col <= row
    start = pl.multiple_of(qi * tq, tq)

    def lane_partial(p):
        out = p[:, 0:LANES]
        for c in range(1, p.shape[1] // LANES):
            out = out + p[:, c * LANES:(c + 1) * LANES]
        return out

    bounds = []
    for mp in range(2):
        qf = qs[mp].astype(F32)
        qn2 = jnp.sum(qf * qf, axis=-1, keepdims=True)
        bounds.append(jnp.sqrt(qn2 * knorm_ref[mp][:, 0:1]))
    kd = k_ref[pl.ds(start, tq), :]
    vd = v_ref[pl.ds(start, tq), :]
    for mp in range(2):
        p = jnp.where(visible, jnp.exp2(_dot_nt(qs[mp], kd[:, ksl[mp]]) - bounds[mp]), 0.0)
        l_ref[mp] = lane_partial(p)
        acc_ref[mp] = _dot(p.astype(BF16), vd)

    def fast_block(kb):
        off = pl.multiple_of(kb * tq, tq)
        kk = k_ref[pl.ds(off, tq), :]
        vv = v_ref[pl.ds(off, tq), :]
        for mp in range(2):
            p = jnp.exp2(_dot_nt(qs[mp], kk[:, ksl[mp]]) - bounds[mp])
            l_ref[mp] += lane_partial(p)
            acc_ref[mp] += _dot(p.astype(BF16), vv)

    def fast_group(n, carry):
        for b in range(DIFF_UNROLL):
            fast_block(DIFF_UNROLL * n + b)
        return carry

    def fast_single(kb, carry):
        fast_block(kb)
        return carry

    n_groups = qi // DIFF_UNROLL
    lax.fori_loop(0, n_groups, fast_group, 0)
    lax.fori_loop(n_groups * DIFF_UNROLL, qi, fast_single, 0)

    l_min = jnp.minimum(jnp.min(jnp.sum(l_ref[0], axis=-1, keepdims=True)),
                        jnp.min(jnp.sum(l_ref[1], axis=-1, keepdims=True)))
    redo = jnp.logical_not(l_min >= DIFF_MIN_NORMALISER)

    @pl.when(redo)
    def _():
        lane0 = lax.broadcasted_iota(jnp.int32, (tq, LANES), 1) == 0
        init = []
        for mp in range(2):
            s = jnp.where(visible, _dot_nt(qs[mp], kd[:, ksl[mp]]), NEG)
            m = jnp.max(s, axis=-1, keepdims=True)
            p = jnp.exp2(s - m)
            acc_ref[mp] = _dot(p.astype(BF16), vd)
            init += [m, jnp.sum(p, axis=-1, keepdims=True)]

        def step(kb, carry):
            off = pl.multiple_of(kb * tq, tq)
            kk = k_ref[pl.ds(off, tq), :]
            vv = v_ref[pl.ds(off, tq), :]
            out = []
            for mp in range(2):
                m, l = carry[2 * mp], carry[2 * mp + 1]
                s = _dot_nt(qs[mp], kk[:, ksl[mp]])
                m_new = jnp.maximum(m, jnp.max(s, axis=-1, keepdims=True))
                alpha = jnp.exp2(m - m_new)
                p = jnp.exp2(s - m_new)
                acc_ref[mp] = alpha * acc_ref[mp] + _dot(p.astype(BF16), vv)
                out += [m_new, alpha * l + jnp.sum(p, axis=-1, keepdims=True)]
            return tuple(out)

        fin = lax.fori_loop(0, qi, step, tuple(init))
        for mp in range(2):
            l_ref[mp] = jnp.where(lane0, fin[2 * mp + 1], 0.0)

    lp = lam_ref[...]
    lam = (jnp.exp(jnp.sum(lp[0:1, :] * lp[1:2, :], axis=-1, keepdims=True))
           - jnp.exp(jnp.sum(lp[2:3, :] * lp[3:4, :], axis=-1, keepdims=True)) + lam_init)
    l0 = jnp.sum(l_ref[0], axis=-1, keepdims=True)
    l1 = jnp.sum(l_ref[1], axis=-1, keepdims=True)
    out = acc_ref[0] / l0 - lam * (acc_ref[1] / l1)
    o_ref[...] = (_rms(out, g_ref[...]) * (1.0 - lam_init)).astype(o_ref.dtype)


def diff_attention(q, k, v, lam_p, g_sub, lam_init, n_heads):
    S = q[0].shape[0]
    W = 2 * HEAD_DIM
    tq = _tile(S, 512)
    return pl.pallas_call(
        functools.partial(_diff_attn_body, tq=tq, lam_init=lam_init),
        grid=(n_heads, S // tq),
        in_specs=[pl.BlockSpec((4, HEAD_DIM), lambda h, i: (0, 0)),
                  pl.BlockSpec((tq, W), lambda h, i: (i, q[1] // W + h)),
                  pl.BlockSpec((S, W), lambda h, i: (0, k[1] // W + h)),
                  pl.BlockSpec((S, W), lambda h, i: (0, v[1] // W + h)),
                  pl.BlockSpec((1, W), lambda h, i: (0, 0))],
        out_specs=pl.BlockSpec((tq, W), lambda h, i: (i, h)),
        out_shape=jax.ShapeDtypeStruct((S, n_heads * W), BF16),
        scratch_shapes=[pltpu.VMEM((2, tq, W), F32), pltpu.VMEM((2, tq, LANES), F32),
                        pltpu.VMEM((2, 1, LANES), F32)],
        compiler_params=_cparams(("parallel", "arbitrary")),
        name="diff_attention",
    )(lam_p, q[0], k[0], v[0], g_sub.reshape(1, W))


def _stick_breaking_body(q_ref, k_ref, v_ref, o_ref, acc_ref, *, tq, heads):
    qi = pl.program_id(1)
    row = lax.broadcasted_iota(jnp.int32, (tq, tq), 0)
    col = lax.broadcasted_iota(jnp.int32, (tq, tq), 1)
    strict = col < row
    suffix_ones = (row >= col).astype(BF16)

    def scores(hh, kb, diag):
        sl = slice(hh * HEAD_DIM, (hh + 1) * HEAD_DIM)
        off = pl.multiple_of(kb * tq, tq)
        z = _dot_nt(q_ref[:, sl], k_ref[pl.ds(off, tq), sl])
        log_keep = -(jnp.maximum(z, 0.0) + jnp.log1p(jnp.exp(-jnp.abs(z))))
        if diag:
            log_keep = jnp.where(strict, log_keep, 0.0)
        hi = log_keep.astype(BF16)
        lo = (log_keep - hi.astype(F32)).astype(BF16)
        cum = _dot(hi, suffix_ones) + _dot(lo, suffix_ones)
        return z + cum, cum[:, 0:1], v_ref[pl.ds(off, tq), sl]

    has_prev = qi > 0
    prev = jnp.maximum(qi - 1, 0)
    runs = []
    for hh in range(heads):
        zc_d, tot_d, v_d = scores(hh, qi, True)
        zc_p, tot_p, v_p = scores(hh, prev, False)
        a_d = jnp.where(strict, jnp.exp(zc_d), 0.0)
        a_p = jnp.where(has_prev, jnp.exp(zc_p + tot_d), 0.0)
        acc_ref[hh] = _dot(a_d.astype(BF16), v_d) + _dot(a_p.astype(BF16), v_p)
        runs.append(tot_d + jnp.where(has_prev, tot_p, 0.0))

    def needs_more(rs):
        return functools.reduce(jnp.maximum, [jnp.max(r) for r in rs]) > SB_LOG_FLOOR

    def cond(state):
        return jnp.logical_and(state[0] >= 0, state[1])

    def step(state):
        kb = state[0]
        new_runs = []
        for hh in range(heads):
            run = state[2 + hh]
            zc, tot, vv = scores(hh, kb, False)
            acc_ref[hh] += _dot(jnp.exp(zc + run).astype(BF16), vv)
            new_runs.append(run + tot)
        return (kb - 1, needs_more(new_runs), *new_runs)

    lax.while_loop(cond, step, (qi - 2, needs_more(runs), *runs))
    for hh in range(heads):
        o_ref[:, hh * HEAD_DIM:(hh + 1) * HEAD_DIM] = acc_ref[hh].astype(o_ref.dtype)


def stick_breaking(q, k, v, n_heads):
    S = q[0].shape[0]
    heads = 2
    W = heads * HEAD_DIM
    tq = _tile(S, 256)
    return pl.pallas_call(
        functools.partial(_stick_breaking_body, tq=tq, heads=heads),
        grid=(n_heads // heads, S // tq),
        in_specs=[pl.BlockSpec((tq, W), lambda h, i: (i, q[1] // W + h)),
                  pl.BlockSpec((S, W), lambda h, i: (0, k[1] // W + h)),
                  pl.BlockSpec((S, W), lambda h, i: (0, v[1] // W + h))],
        out_specs=pl.BlockSpec((tq, W), lambda h, i: (i, h)),
        out_shape=jax.ShapeDtypeStruct((S, n_heads * HEAD_DIM), BF16),
        scratch_shapes=[pltpu.VMEM((heads, tq, HEAD_DIM), F32)],
        compiler_params=_cparams(("parallel", "parallel")),
        name="stick_breaking",
    )(q[0], k[0], v[0])


def _hi_lo(x):
    hi = x.astype(BF16)
    return hi, (x - hi.astype(F32)).astype(BF16)


def _mlstm_body(q_ref, k_ref, v_ref, og_ref, gc_ref, gr_ref, bc_ref, br_ref, gn_ref, o_ref,
                c_ref, n_ref, m_ref, *, n_heads, L):
    dv = 2 * HEAD_DIM

    @pl.when(pl.program_id(0) == 0)
    def _():
        c_ref[...] = jnp.zeros_like(c_ref)
        n_ref[...] = jnp.zeros_like(n_ref)
        m_ref[...] = jnp.zeros_like(m_ref)

    row = lax.broadcasted_iota(jnp.int32, (L, L), 0)
    col = lax.broadcasted_iota(jnp.int32, (L, L), 1)
    tril = col <= row
    tril_ones = tril.astype(BF16)
    triu_ones = (row <= col).astype(BF16)

    gc = gc_ref[...] + bc_ref[...]
    gr = gr_ref[...] + br_ref[...]
    hi, lo = _hi_lo(_log_sigmoid(gc))
    cum_c = _dot(tril_ones, hi) + _dot(tril_ones, lo)
    hi, lo = _hi_lo(_log_sigmoid(gr))
    cum_r = _dot(hi, triu_ones) + _dot(lo, triu_ones)

    gnorm = gn_ref[...]
    for h in range(n_heads):
        b_c = cum_c[:, n_heads + h:n_heads + h + 1]
        li_c = gc[:, h:h + 1]
        b_r = cum_r[n_heads + h:n_heads + h + 1, :]
        li_r = gr[h:h + 1, :]
        m_prev = m_ref[:, h:h + 1]

        dm = jnp.where(tril, b_c - b_r + li_r, NEG)
        inter = b_c + m_prev
        m_t = jnp.maximum(inter, jnp.max(dm, axis=-1, keepdims=True))
        w_intra = jnp.exp(dm - m_t)
        w_inter = jnp.exp(inter - m_t)

        qh = q_ref[:, h * HEAD_DIM:(h + 1) * HEAD_DIM]
        kh = k_ref[:, h * HEAD_DIM:(h + 1) * HEAD_DIM]
        vh = v_ref[:, h * dv:(h + 1) * dv]
        s = _dot_nt(qh, kh) * w_intra
        num = _dot(s.astype(BF16), vh) + w_inter * _dot(qh, c_ref[h].astype(BF16))
        nq = (jnp.sum(s, axis=-1, keepdims=True)
              + w_inter * jnp.sum(qh.astype(F32) * n_ref[h], axis=-1, keepdims=True))
        den = jnp.maximum(jnp.abs(nq), jnp.exp(-m_t))
        hh = num / den

        bl = b_c[L - 1:L, :]
        g_c = bl - b_c + li_c
        m_new = jnp.maximum(bl + m_prev, jnp.max(g_c, axis=0, keepdims=True))
        wk = jnp.exp(g_c - m_new)
        dec = jnp.exp(bl + m_prev - m_new)
        kw = kh.astype(F32) * wk
        c_ref[h] = dec * c_ref[h] + _dot(kw.T.astype(BF16), vh)
        n_ref[h] = dec * n_ref[h] + jnp.sum(kw, axis=0, keepdims=True)
        m_ref[:, h:h + 1] = m_new

        og = og_ref[:, h * dv:(h + 1) * dv].astype(F32)
        o_ref[:, h * dv:(h + 1) * dv] = (_rms(hh, gnorm) * _sigmoid(og)).astype(o_ref.dtype)


def mlstm(q, k, v, og, gates_c, gates_r, bias_c, bias_r, g_norm, n_heads):
    S = q[0].shape[0]
    L = _tile(S, REC_CHUNK)
    qk_w = n_heads * HEAD_DIM
    v_w = n_heads * 2 * HEAD_DIM
    return pl.pallas_call(
        functools.partial(_mlstm_body, n_heads=n_heads, L=L),
        grid=(S // L,),
        in_specs=[pl.BlockSpec((L, qk_w), lambda c: (c, q[1] // qk_w)),
                  pl.BlockSpec((L, qk_w), lambda c: (c, k[1] // qk_w)),
                  pl.BlockSpec((L, v_w), lambda c: (c, v[1] // v_w)),
                  pl.BlockSpec((L, v_w), lambda c: (c, og[1] // v_w)),
                  pl.BlockSpec((L, LANES), lambda c: (c, 0)),
                  pl.BlockSpec((SUBLANES, L), lambda c: (0, c)),
                  pl.BlockSpec((1, LANES), lambda c: (0, 0)),
                  pl.BlockSpec((SUBLANES, 1), lambda c: (0, 0)),
                  pl.BlockSpec((1, 2 * HEAD_DIM), lambda c: (0, 0))],
        out_specs=pl.BlockSpec((L, v_w), lambda c: (c, 0)),
        out_shape=jax.ShapeDtypeStruct((S, v_w), BF16),
        scratch_shapes=[pltpu.VMEM((n_heads, HEAD_DIM, 2 * HEAD_DIM), F32),
                        pltpu.VMEM((n_heads, 1, HEAD_DIM), F32),
                        pltpu.VMEM((1, LANES), F32)],
        compiler_params=_cparams(("arbitrary",)),
        name="mlstm",
    )(q[0], k[0], v[0], og[0], gates_c, gates_r, bias_c, bias_r, g_norm.reshape(1, 2 * HEAD_DIM))


def _retention_body(q_ref, k_ref, v_ref, cg_ref, gn_ref, o_ref, r_ref, dmask_ref, *, n_heads, L):
    dv = 2 * HEAD_DIM
    log_g = [math.log(1.0 - 2.0 ** (-5.0 - h)) for h in range(n_heads)]

    @pl.when(pl.program_id(0) == 0)
    def _():
        r_ref[...] = jnp.zeros_like(r_ref)
        diff = (lax.broadcasted_iota(jnp.int32, (L, L), 0)
                - lax.broadcasted_iota(jnp.int32, (L, L), 1)).astype(F32)
        for h in range(n_heads):
            dmask_ref[h] = jnp.where(diff >= 0, jnp.exp(jnp.maximum(diff, 0.0) * log_g[h]), 0.0)

    idx = lax.broadcasted_iota(jnp.int32, (L, 1), 0).astype(F32)
    gnorm = gn_ref[...]
    for h in range(n_heads):
        q_dec = jnp.exp((idx + 1.0) * log_g[h])
        k_dec = jnp.exp((L - 1.0 - idx) * log_g[h])
        c_dec = math.exp(L * log_g[h])
        qh = q_ref[:, h * HEAD_DIM:(h + 1) * HEAD_DIM]
        kh = k_ref[:, h * HEAD_DIM:(h + 1) * HEAD_DIM]
        vh = v_ref[:, h * dv:(h + 1) * dv]
        s = _dot_nt(qh, kh) * dmask_ref[h]
        o = _dot(s.astype(BF16), vh) + q_dec * _dot(qh, r_ref[h].astype(BF16))
        kd = kh.astype(F32) * k_dec
        r_ref[h] = c_dec * r_ref[h] + _dot(kd.T.astype(BF16), vh)
        cg = cg_ref[:, h * dv:(h + 1) * dv].astype(F32)
        o_ref[:, h * dv:(h + 1) * dv] = (_rms(o, gnorm) * (cg * _sigmoid(cg))).astype(o_ref.dtype)


def retention(q, k, v, cg, g_norm, n_heads):
    S = q[0].shape[0]
    L = _tile(S, REC_CHUNK)
    qk_w = n_heads * HEAD_DIM
    v_w = n_heads * 2 * HEAD_DIM
    return pl.pallas_call(
        functools.partial(_retention_body, n_heads=n_heads, L=L),
        grid=(S // L,),
        in_specs=[pl.BlockSpec((L, qk_w), lambda c: (c, q[1] // qk_w)),
                  pl.BlockSpec((L, qk_w), lambda c: (c, k[1] // qk_w)),
                  pl.BlockSpec((L, v_w), lambda c: (c, v[1] // v_w)),
                  pl.BlockSpec((L, v_w), lambda c: (c, cg[1] // v_w)),
                  pl.BlockSpec((1, 2 * HEAD_DIM), lambda c: (0, 0))],
        out_specs=pl.BlockSpec((L, v_w), lambda c: (c, 0)),
        out_shape=jax.ShapeDtypeStruct((S, v_w), BF16),
        scratch_shapes=[pltpu.VMEM((n_heads, HEAD_DIM, 2 * HEAD_DIM), F32),
                        pltpu.VMEM((n_heads, L, L), F32)],
        compiler_params=_cparams(("arbitrary",)),
        name="retention",
    )(q[0], k[0], v[0], cg[0], g_norm.reshape(1, 2 * HEAD_DIM))


def _rope_tables(S):
    inv = ROPE_THETA ** (-jnp.arange(0, HEAD_DIM, 2, dtype=F32) / HEAD_DIM)
    ang = jnp.arange(S, dtype=F32)[:, None] * inv[None, :]
    cos, sin = jnp.cos(ang), jnp.sin(ang)
    return jnp.concatenate([cos, cos], axis=-1), jnp.concatenate([-sin, sin], axis=-1)


def _col_scale(n, scaled_ranges):
    cs = jnp.ones((1, n), F32)
    for lo, hi, val in scaled_ranges:
        cs = cs.at[:, lo:hi].set(val)
    return cs


def kernel(x, p, w_in_even, w_out_even, diff_lambda, g_diff_sub, w_conv_qk, b_conv_qk, b_igate, b_fgate, g_mlstm_head, w_in_odd, w_out_odd, g_ret_head, g_mix_pre, g_mix_post, g_ffn_pre, g_ffn_post, w_ffn_up, w_ffn_conv, b_ffn_conv, w_ffn_down, g_ple, w_ple_gate_down, w_ple_gate_up, w_ple_proj):
    B, S, D = x.shape
    assert B == 1
    depth = g_mix_pre.shape[0]
    n4 = D // 1024
    n_sb = D // 512
    qk4 = n4 * HEAD_DIM
    v4 = n4 * 2 * HEAD_DIM
    assert 2 * qk4 == v4 == n_sb * HEAD_DIM
    inv_sqrt_d = HEAD_DIM ** -0.5

    cosf, sins = _rope_tables(S)
    cs_even_rope = _col_scale(2 * v4, [(0, v4, inv_sqrt_d * LOG2E)])
    cs_even_conv = _col_scale(v4, [(0, qk4, inv_sqrt_d)])
    cs_odd_rope = _col_scale(v4, [(qk4, 2 * qk4, inv_sqrt_d)])
    cs_odd_plain = _col_scale(5 * v4, [(2 * v4, 3 * v4, inv_sqrt_d)])

    tn = _tile(v4, COL_TILE)
    tiles_v4 = v4 // tn
    p_all = p.reshape(depth * S, p.shape[-1])

    w_in_even_b = w_in_even[:, :, :6 * v4].astype(BF16)
    w_in_odd_b = w_in_odd.astype(BF16)
    w_out_even_b = w_out_even.astype(BF16)
    w_out_odd_b = w_out_odd.astype(BF16)
    w_ffn_up_b = w_ffn_up.astype(BF16)
    w_ffn_down_b = w_ffn_down.astype(BF16)
    w_ple_down_b = w_ple_gate_down.astype(BF16)
    w_ple_up_b = w_ple_gate_up.astype(BF16)
    w_ple_proj_b = w_ple_proj.astype(BF16)

    h = x.reshape(S, D)
    u = first_norm(h, g_mix_pre[0])
    for i in range(depth):
        j = i // 2
        if i % 2 == 0:
            lam_init = 0.8 - 0.6 * math.exp(-0.3 * i)
            w_in = w_in_even[j]
            w_main = (w_in_even_b, j)
            qk_a = rope_proj(u, w_main, cs_even_rope, cosf, sins)
            plain = matmul([(u, w_main, 0)], BF16, n_out=3 * v4, tn=tn,
                           w_col_tile=lambda c: c + 2 * tiles_v4 + jnp.where(c >= tiles_v4, tiles_v4, 0),
                           name="in_proj_plain")
            qk_b = conv_proj(u, w_main, 3, cs_even_conv, w_conv_qk[j], b_conv_qk[j])
            w_gate = w_in[:, 6 * v4:]
            w_gate_c = jnp.pad(w_gate, ((0, 0), (0, LANES - 2 * n4))).astype(BF16)
            w_gate_r = jnp.pad(w_gate.T, ((0, SUBLANES - 2 * n4), (0, 0))).astype(BF16)
            gates_c, gates_r = gate_proj(u, w_gate_c, w_gate_r)
            bias = jnp.concatenate([b_igate[j], b_fgate[j]])
            bias_c = jnp.pad(bias, (0, LANES - 2 * n4)).reshape(1, LANES)
            bias_r = jnp.pad(bias, (0, SUBLANES - 2 * n4)).reshape(SUBLANES, 1)
            ya = diff_attention((qk_a, 0), (qk_a, v4), (plain, 0), diff_lambda[j], g_diff_sub[j], lam_init, n4)
            yb = mlstm((qk_b, 0), (qk_b, qk4), (plain, v4), (plain, 2 * v4),
                       gates_c, gates_r, bias_c, bias_r, g_mlstm_head[j], n4)
            w_out = (w_out_even_b, j)
        else:
            w_main = (w_in_odd_b, j)
            qk_c = rope_proj(u, w_main, cs_odd_rope, cosf, sins)
            plain = matmul([(u, w_main, 0)], BF16, colscale=cs_odd_plain, n_out=5 * v4, tn=tn,
                           w_col_tile=lambda c: c + tiles_v4, name="in_proj_plain")
            ya = retention((qk_c, 0), (qk_c, qk4), (plain, 0), (plain, v4), g_ret_head[j], n4)
            yb = stick_breaking((plain, 2 * v4), (plain, 3 * v4), (plain, 4 * v4), n_sb)
            w_out = (w_out_odd_b, j)
        h, u = out_proj_residual(ya, yb, w_out, h, g_mix_post[i], g_ffn_pre[i])
        hid = ffn_up(u, (w_ffn_up_b, i), w_ffn_conv[i], b_ffn_conv[i])
        f = matmul([(hid, (w_ffn_down_b, i), 0)], F32, name="ffn_down")
        g_next = g_mix_pre[i + 1] if i + 1 < depth else None
        h, u = ffn_residual_ple(h, f, p_all, i, g_ffn_post[i], g_ple[i], w_ple_down_b, w_ple_up_b, w_ple_proj_b,
                                g_next)
    return h.reshape(B, S, D)
```

```python
import functools
import math

import jax
import jax.numpy as jnp
from jax import lax
from jax.experimental import pallas as pl
from jax.experimental.pallas import tpu as pltpu

F32 = jnp.float32
BF16 = jnp.bfloat16

HEAD_DIM = 128
ROPE_THETA = 10000.0
EPS = 1e-6
DIFF_CHUNK = 64
DIFF_UNROLL = 4
REC_CHUNK = 256
NEG = -1e30
LOG2E = math.log2(math.e)
DIFF_MIN_NORMALISER = 2.0 ** -80
SB_LOG_FLOOR = -120.0
VMEM_LIMIT = 56 * 1024 * 1024
LANES = 128
SUBLANES = 8
MXU_COLS = 256
ROW_TILE = 1024
COL_TILE = 1024
NORM_ROWS = 64
ROW_PIECE = 512


def _cparams(sem):
    return pltpu.CompilerParams(dimension_semantics=sem, vmem_limit_bytes=VMEM_LIMIT)


def _dot(a, b):
    return jnp.dot(a, b, preferred_element_type=F32)


def _dot_nt(a, b):
    return lax.dot_general(a, b, (((1,), (1,)), ((), ())), preferred_element_type=F32)


def _rms(x, g):
    return x * lax.rsqrt(jnp.mean(x * x, axis=-1, keepdims=True) + EPS) * g


def _log_sigmoid(x):
    return jnp.minimum(x, 0.0) - jnp.log1p(jnp.exp(-jnp.abs(x)))


def _sigmoid(x):
    return 1.0 / (1.0 + jnp.exp(-x))


def _tile(n, pref):
    if n <= pref:
        return n
    t = pref - pref % MXU_COLS
    while n % t:
        t -= MXU_COLS
    assert t > 0, (n, pref)
    return t


def _layer_spec(block, index_fn, layer):
    return pl.BlockSpec((None,) + block, lambda *g: (layer,) + index_fn(*g))


def _col_subtiles(tn):
    w = min(tn, MXU_COLS)
    assert tn % w == 0
    return [slice(c * w, (c + 1) * w) for c in range(tn // w)]


def _norm_body(x_ref, g_ref, u_ref):
    u_ref[...] = _rms(x_ref[...], g_ref[...]).astype(BF16)


def first_norm(x, g):
    M, D = x.shape
    tm = _tile(M, 256)
    return pl.pallas_call(
        _norm_body,
        grid=(M // tm,),
        in_specs=[pl.BlockSpec((tm, D), lambda i: (i, 0)),
                  pl.BlockSpec((1, D), lambda i: (0, 0))],
        out_specs=pl.BlockSpec((tm, D), lambda i: (i, 0)),
        out_shape=jax.ShapeDtypeStruct((M, D), BF16),
        compiler_params=_cparams(("parallel",)),
        name="first_norm",
    )(x, g.reshape(1, D))


def _ffn_residual_ple_body(h_ref, f_ref, p_ref, gp_ref, gple_ref, wd_ref, wu_ref, wp_ref, gn_ref,
                           ho_ref, *maybe_u_ref):
    h2 = h_ref[...] + _rms(f_ref[...], gp_ref[...])
    t = _dot(_rms(h2, gple_ref[...]).astype(BF16), wd_ref[...])
    gate = _sigmoid(_dot(t.astype(BF16), wu_ref[...]))
    h3 = h2 + gate * _dot(p_ref[...].astype(BF16), wp_ref[...])
    ho_ref[...] = h3
    if maybe_u_ref:
        maybe_u_ref[0][...] = _rms(h3, gn_ref[...]).astype(BF16)


def ffn_residual_ple(h, f, p_all, layer, g_post, g_ple, w_down, w_up, w_proj, g_next):
    M, D = h.shape
    R = w_down.shape[2]
    P = p_all.shape[1]
    tm = _tile(M, 256)
    p_block0 = layer * (M // tm)
    row = pl.BlockSpec((tm, D), lambda i: (i, 0))
    vec = pl.BlockSpec((1, D), lambda i: (0, 0))
    want_u = g_next is not None
    out_specs = [row, row] if want_u else [row]
    out_shape = [jax.ShapeDtypeStruct((M, D), F32)]
    if want_u:
        out_shape.append(jax.ShapeDtypeStruct((M, D), BF16))
    gn = g_next if want_u else g_ple
    res = pl.pallas_call(
        _ffn_residual_ple_body,
        grid=(M // tm,),
        in_specs=[row, row,
                  pl.BlockSpec((tm, P), lambda i: (p_block0 + i, 0)),
                  vec, vec,
                  _layer_spec((D, R), lambda i: (0, 0), layer),
                  _layer_spec((R, D), lambda i: (0, 0), layer),
                  _layer_spec((P, D), lambda i: (0, 0), layer),
                  vec],
        out_specs=out_specs,
        out_shape=out_shape,
        compiler_params=_cparams(("parallel",)),
        name="ffn_residual_ple",
    )(h, f, p_all, g_post.reshape(1, D), g_ple.reshape(1, D), w_down, w_up, w_proj, gn.reshape(1, D))
    return (res[0], res[1]) if want_u else (res[0], None)


def _matmul_body(*refs, n_pairs, scaled):
    o_ref = refs[-1]
    for cols in _col_subtiles(o_ref.shape[1]):
        acc = _dot(refs[0][...], refs[1][:, cols])
        for q in range(1, n_pairs):
            acc = acc + _dot(refs[2 * q][...], refs[2 * q + 1][:, cols])
        if scaled:
            acc = acc * refs[2 * n_pairs][:, cols]
        o_ref[:, cols] = acc.astype(o_ref.dtype)


def matmul(pairs, out_dtype, colscale=None, n_out=None, tn=None, w_col_tile=lambda j: j, name="matmul"):
    M = pairs[0][0].shape[0]
    N = pairs[0][1][0].shape[2] if n_out is None else n_out
    tm = _tile(M, ROW_TILE)
    tn = _tile(N, COL_TILE) if tn is None else tn
    in_specs, args = [], []
    for x, (w_stack, layer), r in pairs:
        K = x.shape[1]
        in_specs += [pl.BlockSpec((tm, K), lambda i, j: (i, 0)),
                     _layer_spec((K, tn), lambda i, j, r=r: (r, w_col_tile(j)), layer)]
        args += [x, w_stack]
    if colscale is not None:
        in_specs.append(pl.BlockSpec((1, tn), lambda i, j: (0, j)))
        args.append(colscale)
    return pl.pallas_call(
        functools.partial(_matmul_body, n_pairs=len(pairs), scaled=colscale is not None),
        grid=(M // tm, N // tn),
        in_specs=in_specs,
        out_specs=pl.BlockSpec((tm, tn), lambda i, j: (i, j)),
        out_shape=jax.ShapeDtypeStruct((M, N), out_dtype),
        compiler_params=_cparams(("parallel", "arbitrary")),
        name=name,
    )(*args)


def _out_proj_residual_body(xa_ref, xb_ref, wa_ref, wb_ref, h_ref, gp_ref, gn_ref, ho_ref, u_ref, y_ref):
    tm = ho_ref.shape[0]
    for cols in _col_subtiles(ho_ref.shape[1]):
        y_ref[:, cols] = _dot(xa_ref[...], wa_ref[:, cols]) + _dot(xb_ref[...], wb_ref[:, cols])
    rc = min(tm, NORM_ROWS)
    for r0 in range(0, tm, rc):
        r = slice(r0, r0 + rc)
        hn = h_ref[r, :] + _rms(y_ref[r, :], gp_ref[...])
        ho_ref[r, :] = hn
        u_ref[r, :] = _rms(hn, gn_ref[...]).astype(u_ref.dtype)


def out_proj_residual(xa, xb, w, h, g_post, g_next):
    M, D = h.shape
    K = xa.shape[1]
    tm = _tile(M, 256)
    row = pl.BlockSpec((tm, D), lambda i: (i, 0))
    vec = pl.BlockSpec((1, D), lambda i: (0, 0))
    xspec = pl.BlockSpec((tm, K), lambda i: (i, 0))

    def wspec(row_block):
        return pl.BlockSpec((None, K, D), lambda i: (w[1], row_block, 0), pipeline_mode=pl.Buffered(1))

    return pl.pallas_call(
        _out_proj_residual_body,
        grid=(M // tm,),
        in_specs=[xspec, xspec, wspec(0), wspec(1), row, vec, vec],
        out_specs=[row, row],
        out_shape=[jax.ShapeDtypeStruct((M, D), F32), jax.ShapeDtypeStruct((M, D), BF16)],
        scratch_shapes=[pltpu.VMEM((tm, D), F32)],
        compiler_params=_cparams(("parallel",)),
        name="out_proj_residual",
    )(xa, xb, w[0], w[0], h, g_post.reshape(1, D), g_next.reshape(1, D))


def _rope_proj_body(u_ref, w_ref, cs_ref, cos_ref, sin_ref, o_ref):
    cos = cos_ref[...]
    sin = sin_ref[...]
    u = u_ref[...]
    for cols in _col_subtiles(o_ref.shape[1]):
        a = _dot(u, w_ref[:, cols]) * cs_ref[:, cols]
        for hh in range((cols.stop - cols.start) // HEAD_DIM):
            x = a[:, hh * HEAD_DIM:(hh + 1) * HEAD_DIM]
            lo = cols.start + hh * HEAD_DIM
            o_ref[:, lo:lo + HEAD_DIM] = (x * cos + pltpu.roll(x, HEAD_DIM // 2, 1) * sin).astype(o_ref.dtype)


def rope_proj(u, w, colscale, cosf, sins):
    M, K = u.shape
    N = colscale.shape[1]
    tm = _tile(M, ROW_TILE)
    tn = _tile(N, COL_TILE)
    return pl.pallas_call(
        _rope_proj_body,
        grid=(M // tm, N // tn),
        in_specs=[pl.BlockSpec((tm, K), lambda i, j: (i, 0)),
                  _layer_spec((K, tn), lambda i, j: (0, j), w[1]),
                  pl.BlockSpec((1, tn), lambda i, j: (0, j)),
                  pl.BlockSpec((tm, HEAD_DIM), lambda i, j: (i, 0)),
                  pl.BlockSpec((tm, HEAD_DIM), lambda i, j: (i, 0))],
        out_specs=pl.BlockSpec((tm, tn), lambda i, j: (i, j)),
        out_shape=jax.ShapeDtypeStruct((M, N), BF16),
        compiler_params=_cparams(("parallel", "arbitrary")),
        name="rope_proj",
    )(u, w[0], colscale, cosf, sins)


def _row_pieces(tm):
    rp = min(tm, ROW_PIECE)
    assert tm % rp == 0
    return [slice(r, r + rp) for r in range(0, tm, rp)]


def _conv_rows(xs_ref, cols, rows, w, b):
    K = w.shape[0]
    lo, hi = SUBLANES + rows.start, SUBLANES + rows.stop
    y = b + w[K - 1:K, :] * xs_ref[lo:hi, cols]
    for d in range(1, K):
        y = y + w[K - 1 - d:K - d, :] * xs_ref[lo - d:hi - d, cols]
    return y


def _staged_conv_pipeline(tm, tn, streams, epilogue):
    row_pieces = _row_pieces(tm)
    blocks = [(cols, rows) for cols in _col_subtiles(tn) for rows in row_pieces]

    def stage(cols, rows):
        for u_ref, w_ref, xs_ref, carry_ref in streams:
            if rows.start == 0:
                xs_ref[0:SUBLANES, cols] = carry_ref[:, cols]
            xs_ref[SUBLANES + rows.start:SUBLANES + rows.stop, cols] = _dot(u_ref[rows, :], w_ref[:, cols])
            if rows.stop == tm:
                carry_ref[:, cols] = xs_ref[tm:tm + SUBLANES, cols]

    stage(*blocks[0])
    for idx, (cols, rows) in enumerate(blocks):
        if idx + 1 < len(blocks):
            stage(*blocks[idx + 1])
        epilogue(cols, rows)


def _conv_proj_body(u_ref, w_ref, cs_ref, cw_ref, cb_ref, o_ref, carry_ref, xs_ref):
    @pl.when(pl.program_id(0) == 0)
    def _():
        carry_ref[...] = jnp.zeros_like(carry_ref)

    def epilogue(cols, rows):
        y = _conv_rows(xs_ref, cols, rows, cw_ref[:, cols], cb_ref[:, cols])
        o_ref[rows, cols] = (y * _sigmoid(y) * cs_ref[:, cols]).astype(o_ref.dtype)

    _staged_conv_pipeline(o_ref.shape[0], o_ref.shape[1], [(u_ref, w_ref, xs_ref, carry_ref)], epilogue)


def conv_proj(u, w, w_col_block, colscale, conv_w, conv_b):
    M, K = u.shape
    N = colscale.shape[1]
    tm = _tile(M, ROW_TILE)
    kc = conv_w.shape[0]
    return pl.pallas_call(
        _conv_proj_body,
        grid=(M // tm,),
        in_specs=[pl.BlockSpec((tm, K), lambda i: (i, 0)),
                  _layer_spec((K, N), lambda i: (0, w_col_block), w[1]),
                  pl.BlockSpec((1, N), lambda i: (0, 0)),
                  pl.BlockSpec((kc, N), lambda i: (0, 0)),
                  pl.BlockSpec((1, N), lambda i: (0, 0))],
        out_specs=pl.BlockSpec((tm, N), lambda i: (i, 0)),
        out_shape=jax.ShapeDtypeStruct((M, N), BF16),
        scratch_shapes=[pltpu.VMEM((SUBLANES, N), F32), pltpu.VMEM((tm + SUBLANES, N), F32)],
        compiler_params=_cparams(("arbitrary",)),
        name="conv_proj",
    )(u, w[0], colscale, conv_w, conv_b.reshape(1, N))


def _gate_proj_body(u_ref, w_ref, wt_ref, o_ref, ot_ref):
    u = u_ref[...]
    o_ref[...] = _dot(u, w_ref[...])
    ot_ref[...] = _dot_nt(wt_ref[...], u)


def gate_proj(u, w_pad, wt_pad):
    M, K = u.shape
    tm = _tile(M, 512)
    return pl.pallas_call(
        _gate_proj_body,
        grid=(M // tm,),
        in_specs=[pl.BlockSpec((tm, K), lambda i: (i, 0)),
                  pl.BlockSpec((K, LANES), lambda i: (0, 0)),
                  pl.BlockSpec((SUBLANES, K), lambda i: (0, 0))],
        out_specs=[pl.BlockSpec((tm, LANES), lambda i: (i, 0)),
                   pl.BlockSpec((SUBLANES, tm), lambda i: (0, i))],
        out_shape=[jax.ShapeDtypeStruct((M, LANES), F32), jax.ShapeDtypeStruct((SUBLANES, M), F32)],
        compiler_params=_cparams(("parallel",)),
        name="gate_proj",
    )(u, w_pad, wt_pad)


def _gelu_tanh(x):
    return 0.5 * x * (1.0 + jnp.tanh(math.sqrt(2.0 / math.pi) * (x + 0.044715 * (x * x * x))))


def _ffn_up_body(u_ref, wa_ref, wb_ref, cwa_ref, cba_ref, cwb_ref, cbb_ref, o_ref,
                 carry_a, carry_b, xs_a, xs_b):
    j = pl.program_id(1)

    @pl.when(pl.program_id(0) == 0)
    def _():
        carry_a[j] = jnp.zeros(carry_a.shape[1:], F32)
        carry_b[j] = jnp.zeros(carry_b.shape[1:], F32)

    def epilogue(cols, rows):
        a = _conv_rows(xs_a, cols, rows, cwa_ref[:, cols], cba_ref[:, cols])
        b = _conv_rows(xs_b, cols, rows, cwb_ref[:, cols], cbb_ref[:, cols])
        o_ref[rows, cols] = (_gelu_tanh(a) * b).astype(o_ref.dtype)

    _staged_conv_pipeline(o_ref.shape[0], o_ref.shape[1],
                          [(u_ref, wa_ref, xs_a, carry_a.at[j]), (u_ref, wb_ref, xs_b, carry_b.at[j])], epilogue)


def ffn_up(u, w_up, conv_w, conv_b):
    M, K = u.shape
    F = w_up[0].shape[2] // 2
    tm = _tile(M, ROW_TILE)
    tn = _tile(F, COL_TILE // 2)
    nj = F // tn
    kc = conv_w.shape[0]
    cb = conv_b.reshape(1, 2 * F)
    return pl.pallas_call(
        _ffn_up_body,
        grid=(M // tm, nj),
        in_specs=[pl.BlockSpec((tm, K), lambda i, j: (i, 0)),
                  _layer_spec((K, tn), lambda i, j: (0, j), w_up[1]),
                  _layer_spec((K, tn), lambda i, j: (0, j + nj), w_up[1]),
                  pl.BlockSpec((kc, tn), lambda i, j: (0, j)),
                  pl.BlockSpec((1, tn), lambda i, j: (0, j)),
                  pl.BlockSpec((kc, tn), lambda i, j: (0, j + nj)),
                  pl.BlockSpec((1, tn), lambda i, j: (0, j + nj))],
        out_specs=pl.BlockSpec((tm, tn), lambda i, j: (i, j)),
        out_shape=jax.ShapeDtypeStruct((M, F), BF16),
        scratch_shapes=[pltpu.VMEM((nj, SUBLANES, tn), F32), pltpu.VMEM((nj, SUBLANES, tn), F32),
                        pltpu.VMEM((tm + SUBLANES, tn), F32), pltpu.VMEM((tm + SUBLANES, tn), F32)],
        compiler_params=_cparams(("arbitrary", "arbitrary")),
        name="ffn_up",
    )(u, w_up[0], w_up[0], conv_w, cb, conv_w, cb)


def _diff_attn_body(lam_ref, q_ref, k_ref, v_ref, g_ref, o_ref, acc_ref, l_ref, knorm_ref, *, tq, lam_init):
    qi = pl.program_id(1)
    n_kv = k_ref.shape[0] // tq
    q = q_ref[...]
    qs = (q[:, :HEAD_DIM], q[:, HEAD_DIM:])
    ksl = (slice(0, HEAD_DIM), slice(HEAD_DIM, 2 * HEAD_DIM))

    @pl.when(qi == 0)
    def _():
        def body(c, carry):
            kk = k_ref[pl.ds(pl.multiple_of(c * tq, tq), tq), :].astype(F32)
            sq = kk * kk
            return tuple(jnp.maximum(carry[mp], jnp.max(jnp.sum(sq[:, ksl[mp]], axis=-1, keepdims=True),
                                                        axis=0, keepdims=True)) for mp in range(2))
        zero = jnp.zeros((1, 1), F32)
        norms = lax.fori_loop(0, n_kv, body, (zero, zero))
        for mp in range(2):
            knorm_ref[mp] = jnp.broadcast_to(norms[mp], (1, LANES))

    row = lax.broadcasted_iota(jnp.int32, (tq, tq), 0) // DIFF_CHUNK
    col = lax.broadcasted_iota(jnp.int32, (tq, tq), 1) // DIFF_CHUNK
    visible = col <= row
    start = pl.multiple_of(qi * tq, tq)

    def lane_partial(p):
        out = p[:, 0:LANES]
        for c in range(1, p.shape[1] // LANES):
            out = out + p[:, c * LANES:(c + 1) * LANES]
        return out

    bounds = []
    for mp in range(2):
        qf = qs[mp].astype(F32)
        qn2 = jnp.sum(qf * qf, axis=-1, keepdims=True)
        bounds.append(jnp.sqrt(qn2 * knorm_ref[mp][:, 0:1]))
    kd = k_ref[pl.ds(start, tq), :]
    vd = v_ref[pl.ds(start, tq), :]
    for mp in range(2):
        p = jnp.where(visible, jnp.exp2(_dot_nt(qs[mp], kd[:, ksl[mp]]) - bounds[mp]), 0.0)
        l_ref[mp] = lane_partial(p)
        acc_ref[mp] = _dot(p.astype(BF16), vd)

    def fast_block(kb):
        off = pl.multiple_of(kb * tq, tq)
        kk = k_ref[pl.ds(off, tq), :]
        vv = v_ref[pl.ds(off, tq), :]
        for mp in range(2):
            p = jnp.exp2(_dot_nt(qs[mp], kk[:, ksl[mp]]) - bounds[mp])
            l_ref[mp] += lane_partial(p)
            acc_ref[mp] += _dot(p.astype(BF16), vv)

    def fast_group(n, carry):
        for b in range(DIFF_UNROLL):
            fast_block(DIFF_UNROLL * n + b)
        return carry

    def fast_single(kb, carry):
        fast_block(kb)
        return carry

    n_groups = qi // DIFF_UNROLL
    lax.fori_loop(0, n_groups, fast_group, 0)
    lax.fori_loop(n_groups * DIFF_UNROLL, qi, fast_single, 0)

    l_min = jnp.minimum(jnp.min(jnp.sum(l_ref[0], axis=-1, keepdims=True)),
                        jnp.min(jnp.sum(l_ref[1], axis=-1, keepdims=True)))
    redo = jnp.logical_not(l_min >= DIFF_MIN_NORMALISER)

    @pl.when(redo)
    def _():
        lane0 = lax.broadcasted_iota(jnp.int32, (tq, LANES), 1) == 0
        init = []
        for mp in range(2):
            s = jnp.where(visible, _dot_nt(qs[mp], kd[:, ksl[mp]]), NEG)
            m = jnp.max(s, axis=-1, keepdims=True)
            p = jnp.exp2(s - m)
            acc_ref[mp] = _dot(p.astype(BF16), vd)
            init += [m, jnp.sum(p, axis=-1, keepdims=True)]

        def step(kb, carry):
            off = pl.multiple_of(kb * tq, tq)
            kk = k_ref[pl.ds(off, tq), :]
            vv = v_ref[pl.ds(off, tq), :]
            out = []
            for mp in range(2):
                m, l = carry[2 * mp], carry[2 * mp + 1]
                s = _dot_nt(qs[mp], kk[:, ksl[mp]])
                m_new = jnp.maximum(m, jnp.max(s, axis=-1, keepdims=True))
                alpha = jnp.exp2(m - m_new)
                p = jnp.exp2(s - m_new)
                acc_ref[mp] = alpha * acc_ref[mp] + _dot(p.astype(BF16), vv)
                out += [m_new, alpha * l + jnp.sum(p, axis=-1, keepdims=True)]
            return tuple(out)

        fin = lax.fori_loop(0, qi, step, tuple(init))
        for mp in range(2):
            l_ref[mp] = jnp.where(lane0, fin[2 * mp + 1], 0.0)

    lp = lam_ref[...]
    lam = (jnp.exp(jnp.sum(lp[0:1, :] * lp[1:2, :], axis=-1, keepdims=True))
           - jnp.exp(jnp.sum(lp[2:3, :] * lp[3:4, :], axis=-1, keepdims=True)) + lam_init)
    l0 = jnp.sum(l_ref[0], axis=-1, keepdims=True)
    l1 = jnp.sum(l_ref[1], axis=-1, keepdims=True)
    out = acc_ref[0] / l0 - lam * (acc_ref[1] / l1)
    o_ref[...] = (_rms(out, g_ref[...]) * (1.0 - lam_init)).astype(o_ref.dtype)


def diff_attention(q, k, v, lam_p, g_sub, lam_init, n_heads):
    S = q[0].shape[0]
    W = 2 * HEAD_DIM
    tq = _tile(S, 512)
    return pl.pallas_call(
        functools.partial(_diff_attn_body, tq=tq, lam_init=lam_init),
        grid=(n_heads, S // tq),
        in_specs=[pl.BlockSpec((4, HEAD_DIM), lambda h, i: (0, 0)),
                  pl.BlockSpec((tq, W), lambda h, i: (i, q[1] // W + h)),
                  pl.BlockSpec((S, W), lambda h, i: (0, k[1] // W + h)),
                  pl.BlockSpec((S, W), lambda h, i: (0, v[1] // W + h)),
                  pl.BlockSpec((1, W), lambda h, i: (0, 0))],
        out_specs=pl.BlockSpec((tq, W), lambda h, i: (i, h)),
        out_shape=jax.ShapeDtypeStruct((S, n_heads * W), BF16),
        scratch_shapes=[pltpu.VMEM((2, tq, W), F32), pltpu.VMEM((2, tq, LANES), F32),
                        pltpu.VMEM((2, 1, LANES), F32)],
        compiler_params=_cparams(("parallel", "arbitrary")),
        name="diff_attention",
    )(lam_p, q[0], k[0], v[0], g_sub.reshape(1, W))


def _stick_breaking_body(q_ref, k_ref, v_ref, o_ref, acc_ref, *, tq, heads):
    qi = pl.program_id(1)
    row = lax.broadcasted_iota(jnp.int32, (tq, tq), 0)
    col = lax.broadcasted_iota(jnp.int32, (tq, tq), 1)
    strict = col < row
    suffix_ones = (row >= col).astype(BF16)

    def scores(hh, kb, diag):
        sl = slice(hh * HEAD_DIM, (hh + 1) * HEAD_DIM)
        off = pl.multiple_of(kb * tq, tq)
        z = _dot_nt(q_ref[:, sl], k_ref[pl.ds(off, tq), sl])
        log_keep = -(jnp.maximum(z, 0.0) + jnp.log1p(jnp.exp(-jnp.abs(z))))
        if diag:
            log_keep = jnp.where(strict, log_keep, 0.0)
        hi = log_keep.astype(BF16)
        lo = (log_keep - hi.astype(F32)).astype(BF16)
        cum = _dot(hi, suffix_ones) + _dot(lo, suffix_ones)
        return z + cum, cum[:, 0:1], v_ref[pl.ds(off, tq), sl]

    has_prev = qi > 0
    prev = jnp.maximum(qi - 1, 0)
    runs = []
    for hh in range(heads):
        zc_d, tot_d, v_d = scores(hh, qi, True)
        zc_p, tot_p, v_p = scores(hh, prev, False)
        a_d = jnp.where(strict, jnp.exp(zc_d), 0.0)
        a_p = jnp.where(has_prev, jnp.exp(zc_p + tot_d), 0.0)
        acc_ref[hh] = _dot(a_d.astype(BF16), v_d) + _dot(a_p.astype(BF16), v_p)
        runs.append(tot_d + jnp.where(has_prev, tot_p, 0.0))

    def needs_more(rs):
        return functools.reduce(jnp.maximum, [jnp.max(r) for r in rs]) > SB_LOG_FLOOR

    def cond(state):
        return jnp.logical_and(state[0] >= 0, state[1])

    def step(state):
        kb = state[0]
        new_runs = []
        for hh in range(heads):
            run = state[2 + hh]
            zc, tot, vv = scores(hh, kb, False)
            acc_ref[hh] += _dot(jnp.exp(zc + run).astype(BF16), vv)
            new_runs.append(run + tot)
        return (kb - 1, needs_more(new_runs), *new_runs)

    lax.while_loop(cond, step, (qi - 2, needs_more(runs), *runs))
    for hh in range(heads):
        o_ref[:, hh * HEAD_DIM:(hh + 1) * HEAD_DIM] = acc_ref[hh].astype(o_ref.dtype)


def stick_breaking(q, k, v, n_heads):
    S = q[0].shape[0]
    heads = 2
    W = heads * HEAD_DIM
    tq = _tile(S, 256)
    return pl.pallas_call(
        functools.partial(_stick_breaking_body, tq=tq, heads=heads),
        grid=(n_heads // heads, S // tq),
        in_specs=[pl.BlockSpec((tq, W), lambda h, i: (i, q[1] // W + h)),
                  pl.BlockSpec((S, W), lambda h, i: (0, k[1] // W + h)),
                  pl.BlockSpec((S, W), lambda h, i: (0, v[1] // W + h))],
        out_specs=pl.BlockSpec((tq, W), lambda h, i: (i, h)),
        out_shape=jax.ShapeDtypeStruct((S, n_heads * HEAD_DIM), BF16),
        scratch_shapes=[pltpu.VMEM((heads, tq, HEAD_DIM), F32)],
        compiler_params=_cparams(("parallel", "parallel")),
        name="stick_breaking",
    )(q[0], k[0], v[0])


def _hi_lo(x):
    hi = x.astype(BF16)
    return hi, (x - hi.astype(F32)).astype(BF16)


def _mlstm_body(q_ref, k_ref, v_ref, og_ref, gc_ref, gr_ref, bc_ref, br_ref, gn_ref, o_ref,
                c_ref, n_ref, m_ref, *, n_heads, L):
    dv = 2 * HEAD_DIM

    @pl.when(pl.program_id(0) == 0)
    def _():
        c_ref[...] = jnp.zeros_like(c_ref)
        n_ref[...] = jnp.zeros_like(n_ref)
        m_ref[...] = jnp.zeros_like(m_ref)

    row = lax.broadcasted_iota(jnp.int32, (L, L), 0)
    col = lax.broadcasted_iota(jnp.int32, (L, L), 1)
    tril = col <= row
    tril_ones = tril.astype(BF16)
    triu_ones = (row <= col).astype(BF16)

    gc = gc_ref[...] + bc_ref[...]
    gr = gr_ref[...] + br_ref[...]
    hi, lo = _hi_lo(_log_sigmoid(gc))
    cum_c = _dot(tril_ones, hi) + _dot(tril_ones, lo)
    hi, lo = _hi_lo(_log_sigmoid(gr))
    cum_r = _dot(hi, triu_ones) + _dot(lo, triu_ones)

    gnorm = gn_ref[...]
    for h in range(n_heads):
        b_c = cum_c[:, n_heads + h:n_heads + h + 1]
        li_c = gc[:, h:h + 1]
        b_r = cum_r[n_heads + h:n_heads + h + 1, :]
        li_r = gr[h:h + 1, :]
        m_prev = m_ref[:, h:h + 1]

        dm = jnp.where(tril, b_c - b_r + li_r, NEG)
        inter = b_c + m_prev
        m_t = jnp.maximum(inter, jnp.max(dm, axis=-1, keepdims=True))
        w_intra = jnp.exp(dm - m_t)
        w_inter = jnp.exp(inter - m_t)

        qh = q_ref[:, h * HEAD_DIM:(h + 1) * HEAD_DIM]
        kh = k_ref[:, h * HEAD_DIM:(h + 1) * HEAD_DIM]
        vh = v_ref[:, h * dv:(h + 1) * dv]
        s = _dot_nt(qh, kh) * w_intra
        num = _dot(s.astype(BF16), vh) + w_inter * _dot(qh, c_ref[h].astype(BF16))
        nq = (jnp.sum(s, axis=-1, keepdims=True)
              + w_inter * jnp.sum(qh.astype(F32) * n_ref[h], axis=-1, keepdims=True))
        den = jnp.maximum(jnp.abs(nq), jnp.exp(-m_t))
        hh = num / den

        bl = b_c[L - 1:L, :]
        g_c = bl - b_c + li_c
        m_new = jnp.maximum(bl + m_prev, jnp.max(g_c, axis=0, keepdims=True))
        wk = jnp.exp(g_c - m_new)
        dec = jnp.exp(bl + m_prev - m_new)
        kw = kh.astype(F32) * wk
        c_ref[h] = dec * c_ref[h] + _dot(kw.T.astype(BF16), vh)
        n_ref[h] = dec * n_ref[h] + jnp.sum(kw, axis=0, keepdims=True)
        m_ref[:, h:h + 1] = m_new

        og = og_ref[:, h * dv:(h + 1) * dv].astype(F32)
        o_ref[:, h * dv:(h + 1) * dv] = (_rms(hh, gnorm) * _sigmoid(og)).astype(o_ref.dtype)


def mlstm(q, k, v, og, gates_c, gates_r, bias_c, bias_r, g_norm, n_heads):
    S = q[0].shape[0]
    L = _tile(S, REC_CHUNK)
    qk_w = n_heads * HEAD_DIM
    v_w = n_heads * 2 * HEAD_DIM
    return pl.pallas_call(
        functools.partial(_mlstm_body, n_heads=n_heads, L=L),
        grid=(S // L,),
        in_specs=[pl.BlockSpec((L, qk_w), lambda c: (c, q[1] // qk_w)),
                  pl.BlockSpec((L, qk_w), lambda c: (c, k[1] // qk_w)),
                  pl.BlockSpec((L, v_w), lambda c: (c, v[1] // v_w)),
                  pl.BlockSpec((L, v_w), lambda c: (c, og[1] // v_w)),
                  pl.BlockSpec((L, LANES), lambda c: (c, 0)),
                  pl.BlockSpec((SUBLANES, L), lambda c: (0, c)),
                  pl.BlockSpec((1, LANES), lambda c: (0, 0)),
                  pl.BlockSpec((SUBLANES, 1), lambda c: (0, 0)),
                  pl.BlockSpec((1, 2 * HEAD_DIM), lambda c: (0, 0))],
        out_specs=pl.BlockSpec((L, v_w), lambda c: (c, 0)),
        out_shape=jax.ShapeDtypeStruct((S, v_w), BF16),
        scratch_shapes=[pltpu.VMEM((n_heads, HEAD_DIM, 2 * HEAD_DIM), F32),
                        pltpu.VMEM((n_heads, 1, HEAD_DIM), F32),
                        pltpu.VMEM((1, LANES), F32)],
        compiler_params=_cparams(("arbitrary",)),
        name="mlstm",
    )(q[0], k[0], v[0], og[0], gates_c, gates_r, bias_c, bias_r, g_norm.reshape(1, 2 * HEAD_DIM))


def _retention_body(q_ref, k_ref, v_ref, cg_ref, gn_ref, o_ref, r_ref, dmask_ref, *, n_heads, L):
    dv = 2 * HEAD_DIM
    log_g = [math.log(1.0 - 2.0 ** (-5.0 - h)) for h in range(n_heads)]

    @pl.when(pl.program_id(0) == 0)
    def _():
        r_ref[...] = jnp.zeros_like(r_ref)
        diff = (lax.broadcasted_iota(jnp.int32, (L, L), 0)
                - lax.broadcasted_iota(jnp.int32, (L, L), 1)).astype(F32)
        for h in range(n_heads):
            dmask_ref[h] = jnp.where(diff >= 0, jnp.exp(jnp.maximum(diff, 0.0) * log_g[h]), 0.0)

    idx = lax.broadcasted_iota(jnp.int32, (L, 1), 0).astype(F32)
    gnorm = gn_ref[...]
    for h in range(n_heads):
        q_dec = jnp.exp((idx + 1.0) * log_g[h])
        k_dec = jnp.exp((L - 1.0 - idx) * log_g[h])
        c_dec = math.exp(L * log_g[h])
        qh = q_ref[:, h * HEAD_DIM:(h + 1) * HEAD_DIM]
        kh = k_ref[:, h * HEAD_DIM:(h + 1) * HEAD_DIM]
        vh = v_ref[:, h * dv:(h + 1) * dv]
        s = _dot_nt(qh, kh) * dmask_ref[h]
        o = _dot(s.astype(BF16), vh) + q_dec * _dot(qh, r_ref[h].astype(BF16))
        kd = kh.astype(F32) * k_dec
        r_ref[h] = c_dec * r_ref[h] + _dot(kd.T.astype(BF16), vh)
        cg = cg_ref[:, h * dv:(h + 1) * dv].astype(F32)
        o_ref[:, h * dv:(h + 1) * dv] = (_rms(o, gnorm) * (cg * _sigmoid(cg))).astype(o_ref.dtype)


def retention(q, k, v, cg, g_norm, n_heads):
    S = q[0].shape[0]
    L = _tile(S, REC_CHUNK)
    qk_w = n_heads * HEAD_DIM
    v_w = n_heads * 2 * HEAD_DIM
    return pl.pallas_call(
        functools.partial(_retention_body, n_heads=n_heads, L=L),
        grid=(S // L,),
        in_specs=[pl.BlockSpec((L, qk_w), lambda c: (c, q[1] // qk_w)),
                  pl.BlockSpec((L, qk_w), lambda c: (c, k[1] // qk_w)),
                  pl.BlockSpec((L, v_w), lambda c: (c, v[1] // v_w)),
                  pl.BlockSpec((L, v_w), lambda c: (c, cg[1] // v_w)),
                  pl.BlockSpec((1, 2 * HEAD_DIM), lambda c: (0, 0))],
        out_specs=pl.BlockSpec((L, v_w), lambda c: (c, 0)),
        out_shape=jax.ShapeDtypeStruct((S, v_w), BF16),
        scratch_shapes=[pltpu.VMEM((n_heads, HEAD_DIM, 2 * HEAD_DIM), F32),
                        pltpu.VMEM((n_heads, L, L), F32)],
        compiler_params=_cparams(("arbitrary",)),
        name="retention",
    )(q[0], k[0], v[0], cg[0], g_norm.reshape(1, 2 * HEAD_DIM))


def _rope_tables(S):
    inv = ROPE_THETA ** (-jnp.arange(0, HEAD_DIM, 2, dtype=F32) / HEAD_DIM)
    ang = jnp.arange(S, dtype=F32)[:, None] * inv[None, :]
    cos, sin = jnp.cos(ang), jnp.sin(ang)
    return jnp.concatenate([cos, cos], axis=-1), jnp.concatenate([-sin, sin], axis=-1)


def _col_scale(n, scaled_ranges):
    cs = jnp.ones((1, n), F32)
    for lo, hi, val in scaled_ranges:
        cs = cs.at[:, lo:hi].set(val)
    return cs


def kernel(x, p, w_in_even, w_out_even, diff_lambda, g_diff_sub, w_conv_qk, b_conv_qk, b_igate, b_fgate, g_mlstm_head, w_in_odd, w_out_odd, g_ret_head, g_mix_pre, g_mix_post, g_ffn_pre, g_ffn_post, w_ffn_up, w_ffn_conv, b_ffn_conv, w_ffn_down, g_ple, w_ple_gate_down, w_ple_gate_up, w_ple_proj):
    B, S, D = x.shape
    assert B == 1
    depth = g_mix_pre.shape[0]
    n4 = D // 1024
    n_sb = D // 512
    qk4 = n4 * HEAD_DIM
    v4 = n4 * 2 * HEAD_DIM
    assert 2 * qk4 == v4 == n_sb * HEAD_DIM
    inv_sqrt_d = HEAD_DIM ** -0.5

    cosf, sins = _rope_tables(S)
    cs_even_rope = _col_scale(2 * v4, [(0, v4, inv_sqrt_d * LOG2E)])
    cs_even_conv = _col_scale(v4, [(0, qk4, inv_sqrt_d)])
    cs_odd_rope = _col_scale(v4, [(qk4, 2 * qk4, inv_sqrt_d)])
    cs_odd_plain = _col_scale(5 * v4, [(2 * v4, 3 * v4, inv_sqrt_d)])

    tn = _tile(v4, COL_TILE)
    tiles_v4 = v4 // tn
    p_all = p.reshape(depth * S, p.shape[-1])

    w_in_even_b = w_in_even.astype(BF16)
    w_in_odd_b = w_in_odd.astype(BF16)
    w_out_even_b = w_out_even.astype(BF16)
    w_out_odd_b = w_out_odd.astype(BF16)
    w_ffn_up_b = w_ffn_up.astype(BF16)
    w_ffn_down_b = w_ffn_down.astype(BF16)
    w_ple_down_b = w_ple_gate_down.astype(BF16)
    w_ple_up_b = w_ple_gate_up.astype(BF16)
    w_ple_proj_b = w_ple_proj.astype(BF16)

    h = x.reshape(S, D)
    u = first_norm(h, g_mix_pre[0])
    for i in range(depth):
        j = i // 2
        if i % 2 == 0:
            lam_init = 0.8 - 0.6 * math.exp(-0.3 * i)
            w_in = w_in_even[j]
            w_main = (w_in_even_b, j)
            qk_a = rope_proj(u, w_main, cs_even_rope, cosf, sins)
            plain = matmul([(u, w_main, 0)], BF16, n_out=3 * v4, tn=tn,
                           w_col_tile=lambda c: c + 2 * tiles_v4 + jnp.where(c >= tiles_v4, tiles_v4, 0),
                           name="in_proj_plain")
            qk_b = conv_proj(u, w_main, 3, cs_even_conv, w_conv_qk[j], b_conv_qk[j])
            w_gate = w_in[:, 6 * v4:]
            w_gate_c = jnp.pad(w_gate, ((0, 0), (0, LANES - 2 * n4))).astype(BF16)
            w_gate_r = jnp.pad(w_gate.T, ((0, SUBLANES - 2 * n4), (0, 0))).astype(BF16)
            gates_c, gates_r = gate_proj(u, w_gate_c, w_gate_r)
            bias = jnp.concatenate([b_igate[j], b_fgate[j]])
            bias_c = jnp.pad(bias, (0, LANES - 2 * n4)).reshape(1, LANES)
            bias_r = jnp.pad(bias, (0, SUBLANES - 2 * n4)).reshape(SUBLANES, 1)
            ya = diff_attention((qk_a, 0), (qk_a, v4), (plain, 0), diff_lambda[j], g_diff_sub[j], lam_init, n4)
            yb = mlstm((qk_b, 0), (qk_b, qk4), (plain, v4), (plain, 2 * v4),
                       gates_c, gates_r, bias_c, bias_r, g_mlstm_head[j], n4)
            w_out = (w_out_even_b, j)
        else:
            w_main = (w_in_odd_b, j)
            qk_c = rope_proj(u, w_main, cs_odd_rope, cosf, sins)
            plain = matmul([(u, w_main, 0)], BF16, colscale=cs_odd_plain, n_out=5 * v4, tn=tn,
                           w_col_tile=lambda c: c + tiles_v4, name="in_proj_plain")
            ya = retention((qk_c, 0), (qk_c, qk4), (plain, 0), (plain, v4), g_ret_head[j], n4)
            yb = stick_breaking((plain, 2 * v4), (plain, 3 * v4), (plain, 4 * v4), n_sb)
            w_out = (w_out_odd_b, j)
        h, u = out_proj_residual(ya, yb, w_out, h, g_mix_post[i], g_ffn_pre[i])
        hid = ffn_up(u, (w_ffn_up_b, i), w_ffn_conv[i], b_ffn_conv[i])
        f = matmul([(hid, (w_ffn_down_b, i), 0)], F32, name="ffn_down")
        g_next = g_mix_pre[i + 1] if i + 1 < depth else None
        h, u = ffn_residual_ple(h, f, p_all, i, g_ffn_post[i], g_ple[i], w_ple_down_b, w_ple_up_b, w_ple_proj_b,
                                g_next)
    return h.reshape(B, S, D)
```
